```python
import jax
import jax.numpy as jnp
from jax import lax
import numpy as np

D_MODEL = 4096
BATCH = 2
SEQ = 4096
DEPTH = 4

N_MIXERS = 4
HEAD_DIM = 128
N_HEADS = D_MODEL // HEAD_DIM
N_KV_HEADS = N_HEADS // 4
GQA_GROUP = N_HEADS // N_KV_HEADS
KV_DIM = N_KV_HEADS * HEAD_DIM
ROPE_THETA = 10000.0
Q_BLOCK = 128
GRID_W = 64
SG_CHUNK = 128
SG_GROUPS = 8
SG_WIDTH = D_MODEL
WINDOW = 128
NA_ROWS_MAX = 8
NA_COLS = 16
N_GROUPS = 8
EXPERTS_PER_GROUP = 4
N_EXPERTS = N_GROUPS * EXPERTS_PER_GROUP
TOP_K_IN_GROUP = 2
D_EXPERT = 384
MOE_BLOCK = 128
RMS_EPS = 1e-6
NEG_INF = -1e30

kernel_name = 'hybrid_interleaved_encoder_hmoe'


def _rms_norm(x, gain):
    xf = x.astype(jnp.float32)
    y = xf * lax.rsqrt(jnp.mean(xf * xf, axis=-1, keepdims=True) + RMS_EPS)
    return (y * gain.astype(jnp.float32)).astype(x.dtype)


def _rope(x, pos):
    dim = x.shape[-1]
    half = dim // 2
    inv = jnp.power(jnp.float32(ROPE_THETA), -jnp.arange(half, dtype=jnp.float32) * (2.0 / dim))
    ang = pos.astype(jnp.float32)[:, None] * inv[None, :]
    cos = jnp.cos(ang).astype(x.dtype)
    sin = jnp.sin(ang).astype(x.dtype)
    x1, x2 = x[..., :half], x[..., half:]
    return jnp.concatenate([x1 * cos - x2 * sin, x2 * cos + x1 * sin], axis=-1)


def _axial_rope(x, row, col):
    half = x.shape[-1] // 2
    return jnp.concatenate([_rope(x[..., :half], row), _rope(x[..., half:], col)], axis=-1)


def _project_qkv(h, w_in):
    b, s, _ = h.shape
    proj = h @ w_in
    q = proj[..., :D_MODEL].reshape(b, s, N_KV_HEADS, GQA_GROUP, HEAD_DIM).transpose(0, 2, 3, 1, 4)
    k = proj[..., D_MODEL:D_MODEL + KV_DIM].reshape(b, s, N_KV_HEADS, HEAD_DIM).transpose(0, 2, 1, 3)
    v = proj[..., D_MODEL + KV_DIM:].reshape(b, s, N_KV_HEADS, HEAD_DIM).transpose(0, 2, 1, 3)
    return q, k, v


def _merge_heads(o, w_o):
    b, _, _, s, _ = o.shape
    return o.transpose(0, 3, 1, 2, 4).reshape(b, s, D_MODEL) @ w_o


def _mixer_axial_gqa(h, w_in, q_gain, k_gain, w_o):
    b, s, _ = h.shape
    q, k, v = _project_qkv(h, w_in)
    q = _rms_norm(q, q_gain)
    k = _rms_norm(k, k_gain)
    t = jnp.arange(s, dtype=jnp.int32)
    row, col = t // GRID_W, t % GRID_W
    q = _axial_rope(q, row, col)
    k = _axial_rope(k, row, col)
    nb = s // Q_BLOCK
    qb = q.reshape(b, N_KV_HEADS, GQA_GROUP, nb, Q_BLOCK, HEAD_DIM).transpose(3, 0, 1, 2, 4, 5)
    scale = HEAD_DIM ** -0.5

    def block(qi):
        sc = jnp.einsum('bhgqd,bhkd->bhgqk', qi, k).astype(jnp.float32) * scale
        p = jax.nn.softmax(sc, axis=-1).astype(v.dtype)
        return jnp.einsum('bhgqk,bhkd->bhgqd', p, v)

    o = lax.map(block, qb)
    o = o.transpose(1, 2, 3, 0, 4, 5).reshape(b, N_KV_HEADS, GQA_GROUP, s, HEAD_DIM)
    return _merge_heads(o, w_o)


def _mixer_chunk_sgu(h, w_in, v_gain, w_s, b_s, w_o):
    b, s, _ = h.shape
    z = jax.nn.gelu(h @ w_in)
    u, v = z[..., :SG_WIDTH], z[..., SG_WIDTH:]
    v = _rms_norm(v, v_gain)
    nc = s // SG_CHUNK
    cg = SG_WIDTH // SG_GROUPS
    vc = v.reshape(b, nc, SG_CHUNK, SG_GROUPS, cg)
    mixed = jnp.einsum('gpq,bcqgd->bcpgd', w_s, vc) + b_s.T[None, None, :, :, None]
    return (u * mixed.reshape(b, s, SG_WIDTH)) @ w_o


def _mixer_window_sink(h, w_in, sink, w_o):
    b, s, _ = h.shape
    q, k, v = _project_qkv(h, w_in)
    pos = jnp.arange(s, dtype=jnp.int32)
    q = _rope(q, pos)
    k = _rope(k, pos)
    nb = s // Q_BLOCK
    qb = q.reshape(b, N_KV_HEADS, GQA_GROUP, nb, Q_BLOCK, HEAD_DIM)
    pad = ((0, 0), (0, 0), (Q_BLOCK, Q_BLOCK), (0, 0))
    kp = jnp.pad(k, pad).reshape(b, N_KV_HEADS, nb + 2, Q_BLOCK, HEAD_DIM)
    vp = jnp.pad(v, pad).reshape(b, N_KV_HEADS, nb + 2, Q_BLOCK, HEAD_DIM)
    kw = jnp.concatenate([kp[:, :, :-2], kp[:, :, 1:-1], kp[:, :, 2:]], axis=3)
    vw = jnp.concatenate([vp[:, :, :-2], vp[:, :, 1:-1], vp[:, :, 2:]], axis=3)
    sc = jnp.einsum('bhgnqd,bhnkd->bhgnqk', qb, kw).astype(jnp.float32) * (HEAD_DIM ** -0.5)
    blk = jnp.arange(nb, dtype=jnp.int32)[:, None]
    qpos = blk * Q_BLOCK + jnp.arange(Q_BLOCK, dtype=jnp.int32)[None, :]
    kpos = blk * Q_BLOCK - Q_BLOCK + jnp.arange(3 * Q_BLOCK, dtype=jnp.int32)[None, :]
    kk = kpos[:, None, :]
    valid = (jnp.abs(qpos[:, :, None] - kk) <= WINDOW) & (kk >= 0) & (kk < s)
    sc = jnp.where(valid, sc, NEG_INF)
    sink_l = sink.reshape(N_KV_HEADS, GQA_GROUP).astype(jnp.float32)[None, :, :, None, None, None]
    m = jnp.maximum(jnp.max(sc, axis=-1, keepdims=True), sink_l)
    p = jnp.exp(sc - m)
    denom = jnp.sum(p, axis=-1, keepdims=True) + jnp.exp(sink_l - m)
    p = (p / denom).astype(v.dtype)
    o = jnp.einsum('bhgnqk,bhnkd->bhgnqd', p, vw).reshape(b, N_KV_HEADS, GQA_GROUP, s, HEAD_DIM)
    return _merge_heads(o, w_o)


def _mixer_neighbourhood(h, w_in, rpb, w_o):
    b, s, _ = h.shape
    rows = s // GRID_W
    kr = min(NA_ROWS_MAX, rows)
    nk = kr * NA_COLS
    q, k, v = _project_qkv(h, w_in)
    t = jnp.arange(s, dtype=jnp.int32)
    r, c = t // GRID_W, t % GRID_W
    rs = jnp.clip(r - kr // 2, 0, rows - kr)
    cs = jnp.clip(c - NA_COLS // 2, 0, GRID_W - NA_COLS)
    key_row = rs[:, None, None] + jnp.arange(kr, dtype=jnp.int32)[None, :, None]
    key_col = cs[:, None, None] + jnp.arange(NA_COLS, dtype=jnp.int32)[None, None, :]
    idx = (key_row * GRID_W + key_col).reshape(s, nk)
    dr = jnp.broadcast_to(key_row - r[:, None, None], (s, kr, NA_COLS)).reshape(s, nk) + (NA_ROWS_MAX - 1)
    dc = jnp.broadcast_to(key_col - c[:, None, None], (s, kr, NA_COLS)).reshape(s, nk) + (NA_COLS - 1)
    nb = s // Q_BLOCK
    qb = q.reshape(b, N_KV_HEADS, GQA_GROUP, nb, Q_BLOCK, HEAD_DIM).transpose(3, 0, 1, 2, 4, 5)
    scale = HEAD_DIM ** -0.5

    def block(args):
        qi, ii, dri, dci = args
        kg = k[:, :, ii]
        vg = v[:, :, ii]
        sc = jnp.einsum('bhgqd,bhqkd->bhgqk', qi, kg).astype(jnp.float32) * scale
        bias = rpb[:, dri, dci].reshape(N_KV_HEADS, GQA_GROUP, Q_BLOCK, nk).astype(jnp.float32)
        p = jax.nn.softmax(sc + bias[None], axis=-1).astype(v.dtype)
        return jnp.einsum('bhgqk,bhqkd->bhgqd', p, vg)

    o = lax.map(block, (qb, idx.reshape(nb, Q_BLOCK, nk), dr.reshape(nb, Q_BLOCK, nk), dc.reshape(nb, Q_BLOCK, nk)))
    o = o.transpose(1, 2, 3, 0, 4, 5).reshape(b, N_KV_HEADS, GQA_GROUP, s, HEAD_DIM)
    return _merge_heads(o, w_o)


def _hierarchical_moe(h, w_group, b_group, w_expert, b_expert, w1, w3, w2):
    b, s, d = h.shape
    n = b * s
    xt = h.reshape(n, d)
    g_logits = (xt @ w_group).astype(jnp.float32) + b_group.astype(jnp.float32)
    g_prob = jax.nn.softmax(g_logits, axis=-1)
    grp = jnp.argmax(g_logits, axis=-1).astype(jnp.int32)
    p_grp = jnp.take_along_axis(g_prob, grp[:, None], axis=-1)
    e_logits = jnp.einsum('nd,gde->nge', xt, w_expert).astype(jnp.float32) + b_expert.astype(jnp.float32)
    e_logits = jnp.take_along_axis(e_logits, grp[:, None, None], axis=1)[:, 0]
    top_p, top_e = lax.top_k(jax.nn.softmax(e_logits, axis=-1), TOP_K_IN_GROUP)
    gates = p_grp * (top_p / jnp.sum(top_p, axis=-1, keepdims=True))
    expert = grp[:, None] * EXPERTS_PER_GROUP + top_e.astype(jnp.int32)
    nka = n * TOP_K_IN_GROUP
    flat_e = expert.reshape(nka)
    flat_g = gates.reshape(nka)
    flat_tok = jnp.repeat(jnp.arange(n, dtype=jnp.int32), TOP_K_IN_GROUP)
    counts = jnp.zeros((N_EXPERTS,), jnp.int32).at[flat_e].add(1)
    padded = ((counts + MOE_BLOCK - 1) // MOE_BLOCK) * MOE_BLOCK
    pad_end = jnp.cumsum(padded)
    pad_start = pad_end - padded
    start = jnp.cumsum(counts) - counts
    order = jnp.argsort(flat_e)
    e_sorted = flat_e[order]
    dest = pad_start[e_sorted] + (jnp.arange(nka, dtype=jnp.int32) - start[e_sorted])
    nb = (nka + MOE_BLOCK - 1) // MOE_BLOCK + N_EXPERTS
    n_rows = nb * MOE_BLOCK
    row_tok = jnp.full((n_rows,), n, jnp.int32).at[dest].set(flat_tok[order])
    row_gate = jnp.zeros((n_rows,), jnp.float32).at[dest].set(flat_g[order])
    block_e = jnp.minimum(jnp.searchsorted(pad_end, jnp.arange(nb, dtype=jnp.int32) * MOE_BLOCK, side='right'), N_EXPERTS - 1).astype(jnp.int32)
    x_pad = jnp.concatenate([xt, jnp.zeros((1, d), xt.dtype)], axis=0)
    xb = x_pad[row_tok].reshape(nb, MOE_BLOCK, d)

    def expert_block(args):
        xi, e = args
        return (jax.nn.silu(xi @ w1[e]) * (xi @ w3[e])) @ w2[e]

    yb = lax.map(expert_block, (xb, block_e)).reshape(n_rows, d)
    y = yb * row_gate[:, None].astype(yb.dtype)
    out = jnp.zeros((n + 1, d), y.dtype).at[row_tok].add(y)
    return out[:n].reshape(b, s, d)


def setup_inputs(seed: int = 0) -> dict:
    key = jax.random.key(seed)
    keys = iter(jax.random.split(key, 32))
    f32 = jnp.float32

    def nrm(shape, scale):
        return jax.random.normal(next(keys), shape, f32) * scale

    n_a, n_b, n_c, n_d = [len(range(m, DEPTH, N_MIXERS)) for m in range(N_MIXERS)]
    d = D_MODEL
    qkv = D_MODEL + 2 * KV_DIM
    return {
        'x': nrm((BATCH, SEQ, d), 1.0),
        'norm_mix': 1.0 + nrm((DEPTH, d), 0.02),
        'norm_ffn': 1.0 + nrm((DEPTH, d), 0.02),
        'norm_final': 1.0 + nrm((d,), 0.02),
        'a_w_in': nrm((n_a, d, qkv), d ** -0.5),
        'a_q_gain': 1.0 + nrm((n_a, HEAD_DIM), 0.02),
        'a_k_gain': 1.0 + nrm((n_a, HEAD_DIM), 0.02),
        'a_w_o': nrm((n_a, d, d), d ** -0.5),
        'b_w_in': nrm((n_b, d, 2 * SG_WIDTH), d ** -0.5),
        'b_v_gain': 1.0 + nrm((n_b, SG_WIDTH), 0.02),
        'b_w_s': nrm((n_b, SG_GROUPS, SG_CHUNK, SG_CHUNK), 0.5 * SG_CHUNK ** -0.5),
        'b_bias': 1.0 + nrm((n_b, SG_GROUPS, SG_CHUNK), 0.02),
        'b_w_o': nrm((n_b, SG_WIDTH, d), SG_WIDTH ** -0.5),
        'c_w_in': nrm((n_c, d, qkv), d ** -0.5),
        'c_sink': nrm((n_c, N_HEADS), 0.5),
        'c_w_o': nrm((n_c, d, d), d ** -0.5),
        'd_w_in': nrm((n_d, d, qkv), d ** -0.5),
        'd_rpb': nrm((n_d, N_HEADS, 2 * NA_ROWS_MAX - 1, 2 * NA_COLS - 1), 0.2),
        'd_w_o': nrm((n_d, d, d), d ** -0.5),
        'moe_w_group': nrm((DEPTH, d, N_GROUPS), d ** -0.5),
        'moe_b_group': nrm((DEPTH, N_GROUPS), 0.01),
        'moe_w_expert': nrm((DEPTH, N_GROUPS, d, EXPERTS_PER_GROUP), d ** -0.5),
        'moe_b_expert': nrm((DEPTH, N_GROUPS, EXPERTS_PER_GROUP), 0.01),
        'moe_w1': nrm((DEPTH, N_EXPERTS, d, D_EXPERT), d ** -0.5),
        'moe_w3': nrm((DEPTH, N_EXPERTS, d, D_EXPERT), d ** -0.5),
        'moe_w2': nrm((DEPTH, N_EXPERTS, D_EXPERT, d), D_EXPERT ** -0.5),
    }


def reference(x, norm_mix, norm_ffn, norm_final,
              a_w_in, a_q_gain, a_k_gain, a_w_o,
              b_w_in, b_v_gain, b_w_s, b_bias, b_w_o,
              c_w_in, c_sink, c_w_o,
              d_w_in, d_rpb, d_w_o,
              moe_w_group, moe_b_group, moe_w_expert, moe_b_expert,
              moe_w1, moe_w3, moe_w2):
    h = x
    for i in range(DEPTH):
        m, j = i % N_MIXERS, i // N_MIXERS
        hn = _rms_norm(h, norm_mix[i])
        if m == 0:
            mix = _mixer_axial_gqa(hn, a_w_in[j], a_q_gain[j], a_k_gain[j], a_w_o[j])
        elif m == 1:
            mix = _mixer_chunk_sgu(hn, b_w_in[j], b_v_gain[j], b_w_s[j], b_bias[j], b_w_o[j])
        elif m == 2:
            mix = _mixer_window_sink(hn, c_w_in[j], c_sink[j], c_w_o[j])
        else:
            mix = _mixer_neighbourhood(hn, d_w_in[j], d_rpb[j], d_w_o[j])
        h = h + mix
        h = h + _hierarchical_moe(_rms_norm(h, norm_ffn[i]), moe_w_group[i], moe_b_group[i],
                                  moe_w_expert[i], moe_b_expert[i],
                                  moe_w1[i], moe_w3[i], moe_w2[i])
    return _rms_norm(h, norm_final)
```

```python
import functools

import jax
import jax.numpy as jnp
from jax import lax
from jax.experimental import pallas as pl
from jax.experimental.pallas import tpu as pltpu

D_MODEL = 4096
BATCH = 2
SEQ = 4096
DEPTH = 4
N_MIXERS = 4
HEAD_DIM = 128
GQA_GROUP = 4
ROPE_THETA = 10000.0
Q_BLOCK = 128
GRID_W = 64
SG_CHUNK = 128
SG_GROUPS = 8
WINDOW = 128
NA_ROWS = 8
NA_COLS = 16
N_GROUPS = 8
EXPERTS_PER_GROUP = 4
N_EXPERTS = N_GROUPS * EXPERTS_PER_GROUP
D_EXPERT = 384
MOE_BLOCK = 128
RMS_EPS = 1e-6
NEG_INF = -1e30

LANES = 128
NA_WIN_ROWS = 10
VMEM_LIMIT = 48 * 1024 * 1024

F32 = jnp.float32
BF16 = jnp.bfloat16


def _params(*sem):
    return pltpu.CompilerParams(dimension_semantics=sem, vmem_limit_bytes=VMEM_LIMIT)


def _rmsnorm_body(x_ref, g_ref, o_ref):
    x = x_ref[...]
    ms = jnp.mean(x * x, axis=-1, keepdims=True)
    o_ref[...] = (x * lax.rsqrt(ms + RMS_EPS) * g_ref[...]).astype(o_ref.dtype)


def _rmsnorm(x, gain, out_dtype):
    n, d = x.shape
    tm = min(256, n)
    return pl.pallas_call(
        _rmsnorm_body,
        grid=(n // tm,),
        in_specs=[pl.BlockSpec((tm, d), lambda i: (i, 0)),
                  pl.BlockSpec((1, d), lambda i: (0, 0))],
        out_specs=pl.BlockSpec((tm, d), lambda i: (i, 0)),
        out_shape=jax.ShapeDtypeStruct((n, d), out_dtype),
        compiler_params=_params("parallel"),
        name="rmsnorm",
    )(x, gain.reshape(1, d))


def _gelu_tanh(x):
    return 0.5 * x * (1.0 + jnp.tanh(0.7978845608028654 * (x + 0.044715 * (x * x * x))))


def _mm_gelu_body(x_ref, w_ref, o_ref):
    acc = jnp.dot(x_ref[...], w_ref[...], preferred_element_type=F32)
    o_ref[...] = _gelu_tanh(acc).astype(o_ref.dtype)


def _mm_res_body(x_ref, w_ref, r_ref, o_ref):
    acc = jnp.dot(x_ref[...], w_ref[...], preferred_element_type=F32)
    o_ref[...] = r_ref[...] + acc


def _swap_halves(y, width):
    if 2 * width == LANES:
        return pltpu.roll(y, width, axis=1)
    lane = lax.broadcasted_iota(jnp.int32, y.shape, 1)
    first = (lane % (2 * width)) < width
    return jnp.where(first, pltpu.roll(y, LANES - width, axis=1), pltpu.roll(y, width, axis=1))


def _mm_qkv_body(x_ref, w_ref, cos_ref, sin_ref, gq_ref, gk_ref, o_ref, *, mode, n_q, n_qk, scale):
    acc = jnp.dot(x_ref[...], w_ref[...], preferred_element_type=F32)
    j = pl.program_id(1)
    heads = acc.shape[1] // HEAD_DIM
    is_q = j < n_q

    @pl.when(j < n_qk)
    def _():
        if mode == "axial":
            gain = jnp.where(is_q, gq_ref[...] * scale, gk_ref[...])
        else:
            gain = jnp.where(is_q, jnp.full((1, HEAD_DIM), scale, F32), jnp.ones((1, HEAD_DIM), F32))
        for t in range(heads):
            a = acc[:, t * HEAD_DIM:(t + 1) * HEAD_DIM]
            if mode == "axial":
                a = a * lax.rsqrt(jnp.mean(a * a, axis=-1, keepdims=True) + RMS_EPS)
            a = a * gain
            if mode == "axial":
                a = a * cos_ref[...] + _swap_halves(a, HEAD_DIM // 4) * sin_ref[...]
            elif mode == "rope":
                a = a * cos_ref[...] + _swap_halves(a, HEAD_DIM // 2) * sin_ref[...]
            o_ref[:, t * HEAD_DIM:(t + 1) * HEAD_DIM] = a.astype(o_ref.dtype)

    @pl.when(j >= n_qk)
    def _():
        o_ref[...] = acc.astype(o_ref.dtype)


def _mm_tiles(m, n):
    return min(1024, m), min(512, n)


def _matmul_gelu(x, w, out_dtype):
    m, k = x.shape
    n = w.shape[1]
    tm, tn = _mm_tiles(m, n)
    return pl.pallas_call(
        _mm_gelu_body,
        grid=(m // tm, n // tn),
        in_specs=[pl.BlockSpec((tm, k), lambda i, j: (i, 0)),
                  pl.BlockSpec((k, tn), lambda i, j: (0, j))],
        out_specs=pl.BlockSpec((tm, tn), lambda i, j: (i, j)),
        out_shape=jax.ShapeDtypeStruct((m, n), out_dtype),
        compiler_params=_params("parallel", "parallel"),
        name="matmul_gelu",
    )(x, w)


def _matmul_residual(x, w, res):
    m, k = x.shape
    n = w.shape[1]
    tm, tn = _mm_tiles(m, n)
    return pl.pallas_call(
        _mm_res_body,
        grid=(m // tm, n // tn),
        in_specs=[pl.BlockSpec((tm, k), lambda i, j: (i, 0)),
                  pl.BlockSpec((k, tn), lambda i, j: (0, j)),
                  pl.BlockSpec((tm, tn), lambda i, j: (i, j))],
        out_specs=pl.BlockSpec((tm, tn), lambda i, j: (i, j)),
        out_shape=jax.ShapeDtypeStruct((m, n), F32),
        compiler_params=_params("parallel", "parallel"),
        name="matmul_residual",
    )(x, w, res)


def _matmul_qkv(x, w, cos, sin, gq, gk, *, mode, seq):
    m, k = x.shape
    n = w.shape[1]
    d_q = D_MODEL
    d_kv = (n - d_q) // 2
    tm = min(1024, seq)
    tn = min(512, d_kv)
    s_blocks = seq // tm
    body = functools.partial(_mm_qkv_body, mode=mode, n_q=d_q // tn, n_qk=(d_q + d_kv) // tn,
                             scale=HEAD_DIM ** -0.5)
    return pl.pallas_call(
        body,
        grid=(m // tm, n // tn),
        in_specs=[pl.BlockSpec((tm, k), lambda i, j: (i, 0)),
                  pl.BlockSpec((k, tn), lambda i, j: (0, j)),
                  pl.BlockSpec((tm, HEAD_DIM), lambda i, j: (i % s_blocks, 0)),
                  pl.BlockSpec((tm, HEAD_DIM), lambda i, j: (i % s_blocks, 0)),
                  pl.BlockSpec((1, HEAD_DIM), lambda i, j: (0, 0)),
                  pl.BlockSpec((1, HEAD_DIM), lambda i, j: (0, 0))],
        out_specs=pl.BlockSpec((tm, tn), lambda i, j: (i, j)),
        out_shape=jax.ShapeDtypeStruct((m, n), BF16),
        compiler_params=_params("parallel", "parallel"),
        name="matmul_qkv_" + mode,
    )(x, w, cos, sin, gq.reshape(1, HEAD_DIM), gk.reshape(1, HEAD_DIM))


def _rope_cos_sin(pos, dim):
    half = dim // 2
    inv = jnp.power(jnp.float32(ROPE_THETA), -jnp.arange(half, dtype=F32) * (2.0 / dim))
    ang = pos.astype(F32)[:, None] * inv[None, :]
    return jnp.cos(ang), jnp.sin(ang)


def _axial_tables(seq):
    t = jnp.arange(seq, dtype=jnp.int32)
    cr, sr = _rope_cos_sin(t // GRID_W, HEAD_DIM // 2)
    cc, sc = _rope_cos_sin(t % GRID_W, HEAD_DIM // 2)
    return (jnp.concatenate([cr, cr, cc, cc], axis=-1),
            jnp.concatenate([-sr, sr, -sc, sc], axis=-1))


def _rope_tables(seq):
    c, s = _rope_cos_sin(jnp.arange(seq, dtype=jnp.int32), HEAD_DIM)
    return jnp.concatenate([c, c], axis=-1), jnp.concatenate([-s, s], axis=-1)


def _stack_heads(q_ref):
    return jnp.concatenate([q_ref[:, g * HEAD_DIM:(g + 1) * HEAD_DIM] for g in range(GQA_GROUP)], axis=0)


def _unstack_heads(o, o_ref):
    tq = o_ref.shape[0]
    for g in range(GQA_GROUP):
        o_ref[:, g * HEAD_DIM:(g + 1) * HEAD_DIM] = o[g * tq:(g + 1) * tq].astype(o_ref.dtype)


def _qk(q, k):
    return lax.dot_general(q, k, (((1,), (1,)), ((), ())), preferred_element_type=F32)


def _attn_dense_body(q_ref, k_ref, v_ref, o_ref):
    s = _qk(_stack_heads(q_ref), k_ref[...])
    m = jnp.max(s, axis=-1, keepdims=True)
    p = jnp.exp(s - m)
    l = jnp.sum(p, axis=-1, keepdims=True)
    o = jnp.dot(p.astype(BF16), v_ref[...], preferred_element_type=F32) / l
    _unstack_heads(o, o_ref)


def _attn_dense(qkv, seq):
    n = qkv.shape[0]
    batch = n // seq
    n_kv = D_MODEL // (GQA_GROUP * HEAD_DIM)
    k_off = D_MODEL // HEAD_DIM
    v_off = k_off + n_kv
    tq = Q_BLOCK
    nq = seq // tq
    gw = GQA_GROUP * HEAD_DIM
    return pl.pallas_call(
        _attn_dense_body,
        grid=(batch, n_kv, nq),
        in_specs=[pl.BlockSpec((tq, gw), lambda b, h, i: (b * nq + i, h)),
                  pl.BlockSpec((seq, HEAD_DIM), lambda b, h, i: (b, k_off + h)),
                  pl.BlockSpec((seq, HEAD_DIM), lambda b, h, i: (b, v_off + h))],
        out_specs=pl.BlockSpec((tq, gw), lambda b, h, i: (b * nq + i, h)),
        out_shape=jax.ShapeDtypeStruct((n, D_MODEL), BF16),
        compiler_params=_params("parallel", "parallel", "parallel"),
        name="attn_dense",
    )(qkv, qkv, qkv)


def _attn_window_body(sink_ref, q_ref, kp_ref, kc_ref, kn_ref, vp_ref, vc_ref, vn_ref, o_ref, *, seq):
    h = pl.program_id(1)
    i = pl.program_id(2)
    tq = q_ref.shape[0]
    k3 = jnp.concatenate([kp_ref[...], kc_ref[...], kn_ref[...]], axis=0)
    v3 = jnp.concatenate([vp_ref[...], vc_ref[...], vn_ref[...]], axis=0)
    s = _qk(_stack_heads(q_ref), k3)
    row = lax.broadcasted_iota(jnp.int32, s.shape, 0)
    col = lax.broadcasted_iota(jnp.int32, s.shape, 1)
    qpos = i * tq + row % tq
    kpos = (i - 1) * tq + col
    valid = (jnp.abs(qpos - kpos) <= WINDOW) & (kpos >= 0) & (kpos < seq)
    s = jnp.where(valid, s, NEG_INF)
    rowg = lax.broadcasted_iota(jnp.int32, (s.shape[0], 1), 0) // tq
    sink = jnp.zeros((s.shape[0], 1), F32)
    for g in range(GQA_GROUP):
        sink = jnp.where(rowg == g, sink_ref[h * GQA_GROUP + g], sink)
    m = jnp.maximum(jnp.max(s, axis=-1, keepdims=True), sink)
    p = jnp.exp(s - m)
    denom = jnp.sum(p, axis=-1, keepdims=True) + jnp.exp(sink - m)
    o = jnp.dot(p.astype(BF16), v3, preferred_element_type=F32) / denom
    _unstack_heads(o, o_ref)


def _attn_window(qkv, sink, seq):
    n = qkv.shape[0]
    batch = n // seq
    n_kv = D_MODEL // (GQA_GROUP * HEAD_DIM)
    k_off = D_MODEL // HEAD_DIM
    v_off = k_off + n_kv
    tq = Q_BLOCK
    nq = seq // tq
    gw = GQA_GROUP * HEAD_DIM

    def kv_spec(off, delta):
        def index(b, h, i, sink_ref):
            return (b * nq + jnp.clip(i + delta, 0, nq - 1), off + h)
        return pl.BlockSpec((tq, HEAD_DIM), index)

    grid_spec = pltpu.PrefetchScalarGridSpec(
        num_scalar_prefetch=1,
        grid=(batch, n_kv, nq),
        in_specs=[pl.BlockSpec((tq, gw), lambda b, h, i, s_: (b * nq + i, h)),
                  kv_spec(k_off, -1), kv_spec(k_off, 0), kv_spec(k_off, 1),
                  kv_spec(v_off, -1), kv_spec(v_off, 0), kv_spec(v_off, 1)],
        out_specs=pl.BlockSpec((tq, gw), lambda b, h, i, s_: (b * nq + i, h)),
    )
    return pl.pallas_call(
        functools.partial(_attn_window_body, seq=seq),
        grid_spec=grid_spec,
        out_shape=jax.ShapeDtypeStruct((n, D_MODEL), BF16),
        compiler_params=_params("parallel", "parallel", "parallel"),
        name="attn_window",
    )(sink, qkv, qkv, qkv, qkv, qkv, qkv, qkv)


def _na_window_start(i, rows):
    return jnp.clip(2 * i - NA_ROWS // 2, 0, rows - NA_WIN_ROWS)


def _attn_nbr_body(q_ref, k_ref, v_ref, bias_ref, o_ref, *, rows):
    i = pl.program_id(2)
    start = pl.multiple_of(_na_window_start(i, rows) * GRID_W, GRID_W)
    nkeys = NA_WIN_ROWS * GRID_W
    kw = k_ref[pl.ds(start, nkeys), :]
    vw = v_ref[pl.ds(start, nkeys), :]
    s = _qk(_stack_heads(q_ref), kw) + bias_ref[0, 0]
    m = jnp.max(s, axis=-1, keepdims=True)
    p = jnp.exp(s - m)
    l = jnp.sum(p, axis=-1, keepdims=True)
    o = jnp.dot(p.astype(BF16), vw, preferred_element_type=F32) / l
    _unstack_heads(o, o_ref)


def _na_geometry_blocks(nq):
    return (0, 1, 2, nq - 2, nq - 1)


def _na_bias_table(rpb, seq):
    rows = seq // GRID_W
    nq = seq // Q_BLOCK
    n_heads = rpb.shape[0]
    tabs = []
    for i_rep in _na_geometry_blocks(nq):
        t = i_rep * Q_BLOCK + jnp.arange(Q_BLOCK, dtype=jnp.int32)
        r, c = (t // GRID_W)[:, None], (t % GRID_W)[:, None]
        ws = min(max(2 * i_rep - NA_ROWS // 2, 0), rows - NA_WIN_ROWS)
        j = jnp.arange(NA_WIN_ROWS * GRID_W, dtype=jnp.int32)[None, :]
        kr, kc = ws + j // GRID_W, j % GRID_W
        rs = jnp.clip(r - NA_ROWS // 2, 0, rows - NA_ROWS)
        cs = jnp.clip(c - NA_COLS // 2, 0, GRID_W - NA_COLS)
        valid = (kr >= rs) & (kr < rs + NA_ROWS) & (kc >= cs) & (kc < cs + NA_COLS)
        dr = jnp.clip(kr - r + (NA_ROWS - 1), 0, 2 * NA_ROWS - 2)
        dc = jnp.clip(kc - c + (NA_COLS - 1), 0, 2 * NA_COLS - 2)
        tabs.append(jnp.where(valid[None], rpb[:, dr, dc].astype(F32), NEG_INF))
    tab = jnp.stack(tabs)
    return tab.reshape(len(tabs), n_heads // GQA_GROUP, GQA_GROUP * Q_BLOCK, NA_WIN_ROWS * GRID_W)


def _attn_nbr(qkv, rpb, seq):
    n = qkv.shape[0]
    batch = n // seq
    n_kv = D_MODEL // (GQA_GROUP * HEAD_DIM)
    k_off = D_MODEL // HEAD_DIM
    v_off = k_off + n_kv
    tq = Q_BLOCK
    nq = seq // tq
    gw = GQA_GROUP * HEAD_DIM
    rows = seq // GRID_W
    assert rows >= NA_WIN_ROWS and nq >= 5
    bias = _na_bias_table(rpb, seq)
    nkeys = NA_WIN_ROWS * GRID_W

    def geom(i):
        return jnp.where(i < 2, i, jnp.where(i < nq - 2, 2, i - (nq - 5)))

    return pl.pallas_call(
        functools.partial(_attn_nbr_body, rows=rows),
        grid=(n_kv, batch, nq),
        in_specs=[pl.BlockSpec((tq, gw), lambda h, b, i: (b * nq + i, h)),
                  pl.BlockSpec((seq, HEAD_DIM), lambda h, b, i: (b, k_off + h)),
                  pl.BlockSpec((seq, HEAD_DIM), lambda h, b, i: (b, v_off + h)),
                  pl.BlockSpec((1, 1, GQA_GROUP * tq, nkeys), lambda h, b, i: (geom(i), h, 0, 0))],
        out_specs=pl.BlockSpec((tq, gw), lambda h, b, i: (b * nq + i, h)),
        out_shape=jax.ShapeDtypeStruct((n, D_MODEL), BF16),
        compiler_params=_params("parallel", "parallel", "parallel"),
        name="attn_nbr",
    )(qkv, qkv, qkv, bias)


def _sgu_body(u_ref, v_ref, vg_ref, ws_ref, bs_ref, o_ref):
    v = v_ref[...]
    ms = jnp.mean(v * v, axis=-1, keepdims=True)
    vn = (v * lax.rsqrt(ms + RMS_EPS) * vg_ref[...]).astype(BF16)
    cg = v.shape[1] // SG_GROUPS
    for g in range(SG_GROUPS):
        mixed = jnp.dot(ws_ref[g], vn[:, g * cg:(g + 1) * cg], preferred_element_type=F32)
        mixed = mixed + bs_ref[:, g:g + 1]
        o_ref[:, g * cg:(g + 1) * cg] = (u_ref[:, g * cg:(g + 1) * cg] * mixed).astype(o_ref.dtype)


def _sgu(z, v_gain, w_s, b_s):
    n = z.shape[0]
    width = z.shape[1] // 2
    return pl.pallas_call(
        _sgu_body,
        grid=(n // SG_CHUNK,),
        in_specs=[pl.BlockSpec((SG_CHUNK, width), lambda c: (c, 0)),
                  pl.BlockSpec((SG_CHUNK, width), lambda c: (c, 1)),
                  pl.BlockSpec((1, width), lambda c: (0, 0)),
                  pl.BlockSpec((SG_GROUPS, SG_CHUNK, SG_CHUNK), lambda c: (0, 0, 0)),
                  pl.BlockSpec((SG_CHUNK, SG_GROUPS), lambda c: (0, 0))],
        out_specs=pl.BlockSpec((SG_CHUNK, width), lambda c: (c, 0)),
        out_shape=jax.ShapeDtypeStruct((n, width), BF16),
        compiler_params=_params("parallel"),
        name="sgu",
    )(z, z, v_gain.reshape(1, width), w_s.astype(BF16), b_s.T)


def _router_body(x_ref, g_ref, wr_ref, br_ref, xn_ref, ids_ref, gates_ref):
    x = x_ref[...]
    ms = jnp.mean(x * x, axis=-1, keepdims=True)
    xn = x * lax.rsqrt(ms + RMS_EPS) * g_ref[...]
    xn_ref[...] = xn
    logits = jnp.dot(xn, wr_ref[...], preferred_element_type=F32,
                     precision=lax.Precision.HIGHEST) + br_ref[...]
    lane = lax.broadcasted_iota(jnp.int32, logits.shape, 1)
    ninf = jnp.float32(-jnp.inf)

    lg = jnp.where(lane < N_GROUPS, logits, ninf)
    mg = jnp.max(lg, axis=-1, keepdims=True)
    grp = jnp.min(jnp.where(lg == mg, lane, LANES), axis=-1, keepdims=True)
    p_grp = 1.0 / jnp.sum(jnp.exp(lg - mg), axis=-1, keepdims=True)

    lo = N_GROUPS + grp * EXPERTS_PER_GROUP
    le = jnp.where((lane >= lo) & (lane < lo + EXPERTS_PER_GROUP), logits, ninf)
    m1 = jnp.max(le, axis=-1, keepdims=True)
    i1 = jnp.min(jnp.where(le == m1, lane, LANES), axis=-1, keepdims=True)
    le2 = jnp.where(lane == i1, ninf, le)
    m2 = jnp.max(le2, axis=-1, keepdims=True)
    i2 = jnp.min(jnp.where(le2 == m2, lane, LANES), axis=-1, keepdims=True)
    z = jnp.sum(jnp.exp(le - m1), axis=-1, keepdims=True)
    p1 = 1.0 / z
    p2 = jnp.exp(m2 - m1) / z
    tot = p1 + p2
    g1 = p_grp * (p1 / tot)
    g2 = p_grp * (p2 / tot)
    ids_ref[...] = jnp.where(lane == 0, i1 - N_GROUPS, jnp.where(lane == 1, i2 - N_GROUPS, 0))
    gates_ref[...] = jnp.where(lane == 0, g1, jnp.where(lane == 1, g2, 0.0))


def _router(h, gain, w_group, b_group, w_expert, b_expert):
    n, d = h.shape
    tm = min(256, n)
    wr = jnp.concatenate([w_group, w_expert.transpose(1, 0, 2).reshape(d, N_EXPERTS)], axis=1)
    wr = jnp.pad(wr.astype(F32), ((0, 0), (0, LANES - wr.shape[1])))
    br = jnp.concatenate([b_group, b_expert.reshape(N_EXPERTS)]).astype(F32)
    br = jnp.pad(br, (0, LANES - br.shape[0])).reshape(1, LANES)
    return pl.pallas_call(
        _router_body,
        grid=(n // tm,),
        in_specs=[pl.BlockSpec((tm, d), lambda i: (i, 0)),
                  pl.BlockSpec((1, d), lambda i: (0, 0)),
                  pl.BlockSpec((d, LANES), lambda i: (0, 0)),
                  pl.BlockSpec((1, LANES), lambda i: (0, 0))],
        out_specs=[pl.BlockSpec((tm, d), lambda i: (i, 0)),
                   pl.BlockSpec((tm, LANES), lambda i: (i, 0)),
                   pl.BlockSpec((tm, LANES), lambda i: (i, 0))],
        out_shape=[jax.ShapeDtypeStruct((n, d), F32),
                   jax.ShapeDtypeStruct((n, LANES), jnp.int32),
                   jax.ShapeDtypeStruct((n, LANES), F32)],
        compiler_params=_params("parallel"),
        name="norm_router",
    )(h, gain.reshape(1, d), wr, br)


def _dispatch_plan(ids):
    n = ids.shape[0]
    nka = n * 2
    flat_e = ids.reshape(nka)
    onehot = (flat_e[:, None] == jnp.arange(N_EXPERTS, dtype=jnp.int32)[None, :]).astype(jnp.int32)
    csum = jnp.cumsum(onehot, axis=0)
    rank = jnp.sum(csum * onehot, axis=1) - 1
    counts = csum[-1]
    padded = ((counts + MOE_BLOCK - 1) // MOE_BLOCK) * MOE_BLOCK
    pad_end = jnp.cumsum(padded)
    pad_start = pad_end - padded
    dest = (pad_start[flat_e] + rank).astype(jnp.int32)
    nb = (nka + MOE_BLOCK - 1) // MOE_BLOCK + N_EXPERTS
    flat_tok = jnp.arange(nka, dtype=jnp.int32) // 2
    row_tok = jnp.zeros((nb * MOE_BLOCK,), jnp.int32).at[dest].set(flat_tok)
    block_e = jnp.minimum(
        jnp.searchsorted(pad_end, jnp.arange(nb, dtype=jnp.int32) * MOE_BLOCK, side="right"),
        N_EXPERTS - 1).astype(jnp.int32)
    n_used = (pad_end[-1] // MOE_BLOCK).astype(jnp.int32).reshape(1)
    return dest, row_tok, block_e, n_used, nb


def _row_copy(src_hbm, row, dst, dst_row, sem):
    return pltpu.make_async_copy(src_hbm.at[pl.ds(row, 1), :], dst.at[pl.ds(dst_row, 1), :], sem)


def _experts_body(be_ref, tok_ref, nu_ref, x_hbm, w1_ref, w3_ref, w2_ref, o_ref, xbuf, sem):
    i = pl.program_id(0)
    n_used = nu_ref[0]
    rows = xbuf.shape[1]

    def start_gather(blk, slot):
        def body(r, c):
            _row_copy(x_hbm, tok_ref[blk * rows + r], xbuf.at[slot], r, sem.at[slot]).start()
            return c
        lax.fori_loop(0, rows, body, 0)

    def wait_gather(slot):
        def body(r, c):
            _row_copy(x_hbm, 0, xbuf.at[slot], r, sem.at[slot]).wait()
            return c
        lax.fori_loop(0, rows, body, 0)

    @pl.when(i < n_used)
    def _():
        slot = i % 2

        @pl.when(i == 0)
        def _():
            start_gather(0, 0)

        @pl.when(i + 1 < n_used)
        def _():
            start_gather(i + 1, 1 - slot)

        wait_gather(slot)
        x = xbuf[slot].astype(BF16)
        h1 = jnp.dot(x, w1_ref[0], preferred_element_type=F32)
        h3 = jnp.dot(x, w3_ref[0], preferred_element_type=F32)
        a = (h1 / (1.0 + jnp.exp(-h1))) * h3
        o_ref[...] = jnp.dot(a.astype(BF16), w2_ref[0], preferred_element_type=F32)

    @pl.when(i >= n_used)
    def _():
        o_ref[...] = jnp.zeros(o_ref.shape, o_ref.dtype)


def _experts(xn, row_tok, block_e, n_used, nb, w1, w3, w2):
    n, d = xn.shape
    de = w1.shape[2]
    grid_spec = pltpu.PrefetchScalarGridSpec(
        num_scalar_prefetch=3,
        grid=(nb,),
        in_specs=[pl.BlockSpec(memory_space=pl.ANY),
                  pl.BlockSpec((1, d, de), lambda i, be, tok, nu: (be[i], 0, 0)),
                  pl.BlockSpec((1, d, de), lambda i, be, tok, nu: (be[i], 0, 0)),
                  pl.BlockSpec((1, de, d), lambda i, be, tok, nu: (be[i], 0, 0))],
        out_specs=pl.BlockSpec((MOE_BLOCK, d), lambda i, be, tok, nu: (i, 0)),
        scratch_shapes=[pltpu.VMEM((2, MOE_BLOCK, d), F32),
                        pltpu.SemaphoreType.DMA((2,))],
    )
    return pl.pallas_call(
        _experts_body,
        grid_spec=grid_spec,
        out_shape=jax.ShapeDtypeStruct((nb * MOE_BLOCK, d), F32),
        compiler_params=_params("arbitrary"),
        name="experts",
    )(block_e, row_tok, n_used, xn, w1, w3, w2)


def _combine_body(pos_ref, h_ref, g_ref, y_hbm, o_ref, ybuf, sem):
    i = pl.program_id(0)
    n_steps = pl.num_programs(0)
    tm = h_ref.shape[0]

    def start_gather(step, slot):
        def body(r, c):
            base = (step * tm + r) * 2
            _row_copy(y_hbm, pos_ref[base], ybuf.at[slot, 0], r, sem.at[slot]).start()
            _row_copy(y_hbm, pos_ref[base + 1], ybuf.at[slot, 1], r, sem.at[slot]).start()
            return c
        lax.fori_loop(0, tm, body, 0)

    def wait_gather(slot):
        def body(r, c):
            _row_copy(y_hbm, 0, ybuf.at[slot, 0], r, sem.at[slot]).wait()
            _row_copy(y_hbm, 0, ybuf.at[slot, 1], r, sem.at[slot]).wait()
            return c
        lax.fori_loop(0, tm, body, 0)

    slot = i % 2

    @pl.when(i == 0)
    def _():
        start_gather(0, 0)

    @pl.when(i + 1 < n_steps)
    def _():
        start_gather(i + 1, 1 - slot)

    wait_gather(slot)
    g = g_ref[...]
    o_ref[...] = h_ref[...] + (g[:, 0:1] * ybuf[slot, 0] + g[:, 1:2] * ybuf[slot, 1])


def _combine(h, gates, pos, yb):
    n, d = h.shape
    tm = min(128, n)
    grid_spec = pltpu.PrefetchScalarGridSpec(
        num_scalar_prefetch=1,
        grid=(n // tm,),
        in_specs=[pl.BlockSpec((tm, d), lambda i, pos_: (i, 0)),
                  pl.BlockSpec((tm, LANES), lambda i, pos_: (i, 0)),
                  pl.BlockSpec(memory_space=pl.ANY)],
        out_specs=pl.BlockSpec((tm, d), lambda i, pos_: (i, 0)),
        scratch_shapes=[pltpu.VMEM((2, 2, tm, d), F32),
                        pltpu.SemaphoreType.DMA((2,))],
    )
    return pl.pallas_call(
        _combine_body,
        grid_spec=grid_spec,
        out_shape=jax.ShapeDtypeStruct((n, d), F32),
        compiler_params=_params("arbitrary"),
        name="moe_combine",
    )(pos, h, gates, yb)


def _hierarchical_moe(h, gain, w_group, b_group, w_expert, b_expert, w1, w3, w2):
    xn, ids, gates = _router(h, gain, w_group, b_group, w_expert, b_expert)
    dest, row_tok, block_e, n_used, nb = _dispatch_plan(ids[:, :2])
    yb = _experts(xn, row_tok, block_e, n_used, nb, w1.astype(BF16), w3.astype(BF16), w2.astype(BF16))
    return _combine(h, gates, dest, yb)


def kernel(x, norm_mix, norm_ffn, norm_final, a_w_in, a_q_gain, a_k_gain, a_w_o, b_w_in, b_v_gain, b_w_s, b_bias, b_w_o, c_w_in, c_sink, c_w_o, d_w_in, d_rpb, d_w_o, moe_w_group, moe_b_group, moe_w_expert, moe_b_expert, moe_w1, moe_w3, moe_w2):
    batch, seq, d = x.shape
    h = x.reshape(batch * seq, d)
    ones = jnp.ones((HEAD_DIM,), F32)
    for i in range(norm_mix.shape[0]):
        m, j = i % N_MIXERS, i // N_MIXERS
        hn = _rmsnorm(h, norm_mix[i], BF16)
        if m == 0:
            cos, sin = _axial_tables(seq)
            qkv = _matmul_qkv(hn, a_w_in[j].astype(BF16), cos, sin, a_q_gain[j], a_k_gain[j],
                              mode="axial", seq=seq)
            mixed, w_o = _attn_dense(qkv, seq), a_w_o[j]
        elif m == 1:
            z = _matmul_gelu(hn, b_w_in[j].astype(BF16), F32)
            mixed, w_o = _sgu(z, b_v_gain[j], b_w_s[j], b_bias[j]), b_w_o[j]
        elif m == 2:
            cos, sin = _rope_tables(seq)
            qkv = _matmul_qkv(hn, c_w_in[j].astype(BF16), cos, sin, ones, ones, mode="rope", seq=seq)
            mixed, w_o = _attn_window(qkv, c_sink[j].astype(F32), seq), c_w_o[j]
        else:
            cos, sin = _rope_tables(seq)
            qkv = _matmul_qkv(hn, d_w_in[j].astype(BF16), cos, sin, ones, ones, mode="none", seq=seq)
            mixed, w_o = _attn_nbr(qkv, d_rpb[j], seq), d_w_o[j]
        h = _matmul_residual(mixed, w_o.astype(BF16), h)
        h = _hierarchical_moe(h, norm_ffn[i], moe_w_group[i], moe_b_group[i], moe_w_expert[i],
                              moe_b_expert[i], moe_w1[i], moe_w3[i], moe_w2[i])
    return _rmsnorm(h, norm_final, F32).reshape(batch, seq, d)
```

```python
import functools

import jax
import jax.numpy as jnp
from jax import lax
from jax.experimental import pallas as pl
from jax.experimental.pallas import tpu as pltpu

D_MODEL = 4096
BATCH = 2
SEQ = 4096
DEPTH = 4
N_MIXERS = 4
HEAD_DIM = 128
GQA_GROUP = 4
ROPE_THETA = 10000.0
Q_BLOCK = 128
GRID_W = 64
SG_CHUNK = 128
SG_GROUPS = 8
WINDOW = 128
NA_ROWS = 8
NA_COLS = 16
N_GROUPS = 8
EXPERTS_PER_GROUP = 4
N_EXPERTS = N_GROUPS * EXPERTS_PER_GROUP
D_EXPERT = 384
MOE_BLOCK = 128
RMS_EPS = 1e-6
NEG_INF = -1e30

LOG2_E = 1.4426950408889634
LANES = 128
DENSE_KEY_CHUNK = 512
NA_WIN_ROWS = 10
VMEM_LIMIT = 48 * 1024 * 1024

F32 = jnp.float32
BF16 = jnp.bfloat16


def _params(*sem):
    return pltpu.CompilerParams(dimension_semantics=sem, vmem_limit_bytes=VMEM_LIMIT)


def _rmsnorm_body(x_ref, g_ref, o_ref):
    x = x_ref[...]
    ms = jnp.mean(x * x, axis=-1, keepdims=True)
    o_ref[...] = (x * lax.rsqrt(ms + RMS_EPS) * g_ref[...]).astype(o_ref.dtype)


def _rmsnorm(x, gain, out_dtype):
    n, d = x.shape
    tm = min(256, n)
    return pl.pallas_call(
        _rmsnorm_body,
        grid=(n // tm,),
        in_specs=[pl.BlockSpec((tm, d), lambda i: (i, 0)),
                  pl.BlockSpec((1, d), lambda i: (0, 0))],
        out_specs=pl.BlockSpec((tm, d), lambda i: (i, 0)),
        out_shape=jax.ShapeDtypeStruct((n, d), out_dtype),
        compiler_params=_params("parallel"),
        name="rmsnorm",
    )(x, gain.reshape(1, d))


def _gelu_tanh(x):
    return 0.5 * x * (1.0 + jnp.tanh(0.7978845608028654 * (x + 0.044715 * (x * x * x))))


def _mm_gelu_body(x_ref, w_ref, o_ref):
    acc = jnp.dot(x_ref[...], w_ref[...], preferred_element_type=F32)
    o_ref[...] = _gelu_tanh(acc).astype(o_ref.dtype)


def _mm_res_body(x_ref, w_ref, r_ref, o_ref):
    acc = jnp.dot(x_ref[...], w_ref[...], preferred_element_type=F32)
    o_ref[...] = r_ref[...] + acc


def _swap_halves(y, width):
    if 2 * width == LANES:
        return pltpu.roll(y, width, axis=1)
    lane = lax.broadcasted_iota(jnp.int32, y.shape, 1)
    first = (lane % (2 * width)) < width
    return jnp.where(first, pltpu.roll(y, LANES - width, axis=1), pltpu.roll(y, width, axis=1))


def _mm_qkv_body(x_ref, w_ref, cos_ref, sin_ref, gq_ref, gk_ref, o_ref, *, mode, n_q, n_qk, scale):
    acc = jnp.dot(x_ref[...], w_ref[...], preferred_element_type=F32)
    j = pl.program_id(1)
    heads = acc.shape[1] // HEAD_DIM
    is_q = j < n_q

    @pl.when(j < n_qk)
    def _():
        if mode == "axial":
            gain = jnp.where(is_q, gq_ref[...] * scale, gk_ref[...])
        else:
            gain = jnp.where(is_q, jnp.full((1, HEAD_DIM), scale, F32), jnp.ones((1, HEAD_DIM), F32))
        for t in range(heads):
            a = acc[:, t * HEAD_DIM:(t + 1) * HEAD_DIM]
            if mode == "axial":
                a = a * lax.rsqrt(jnp.mean(a * a, axis=-1, keepdims=True) + RMS_EPS)
            a = a * gain
            if mode == "axial":
                a = a * cos_ref[...] + _swap_halves(a, HEAD_DIM // 4) * sin_ref[...]
            elif mode == "rope":
                a = a * cos_ref[...] + _swap_halves(a, HEAD_DIM // 2) * sin_ref[...]
            o_ref[:, t * HEAD_DIM:(t + 1) * HEAD_DIM] = a.astype(o_ref.dtype)

    @pl.when(j >= n_qk)
    def _():
        o_ref[...] = acc.astype(o_ref.dtype)


def _mm_tiles(m, n):
    return min(1024, m), min(512, n)


def _matmul_gelu(x, w, out_dtype):
    m, k = x.shape
    n = w.shape[1]
    tm, tn = _mm_tiles(m, n)
    return pl.pallas_call(
        _mm_gelu_body,
        grid=(m // tm, n // tn),
        in_specs=[pl.BlockSpec((tm, k), lambda i, j: (i, 0)),
                  pl.BlockSpec((k, tn), lambda i, j: (0, j))],
        out_specs=pl.BlockSpec((tm, tn), lambda i, j: (i, j)),
        out_shape=jax.ShapeDtypeStruct((m, n), out_dtype),
        compiler_params=_params("parallel", "parallel"),
        name="matmul_gelu",
    )(x, w)


def _matmul_residual(x, w, res):
    m, k = x.shape
    n = w.shape[1]
    tm, tn = _mm_tiles(m, n)
    return pl.pallas_call(
        _mm_res_body,
        grid=(m // tm, n // tn),
        in_specs=[pl.BlockSpec((tm, k), lambda i, j: (i, 0)),
                  pl.BlockSpec((k, tn), lambda i, j: (0, j)),
                  pl.BlockSpec((tm, tn), lambda i, j: (i, j))],
        out_specs=pl.BlockSpec((tm, tn), lambda i, j: (i, j)),
        out_shape=jax.ShapeDtypeStruct((m, n), F32),
        compiler_params=_params("parallel", "parallel"),
        name="matmul_residual",
    )(x, w, res)


def _matmul_qkv(x, w, cos, sin, gq, gk, *, mode, seq):
    m, k = x.shape
    n = w.shape[1]
    d_q = D_MODEL
    d_kv = (n - d_q) // 2
    tm = min(1024, seq)
    tn = min(512, d_kv)
    s_blocks = seq // tm
    scale = HEAD_DIM ** -0.5 * (LOG2_E if mode == "axial" else 1.0)
    body = functools.partial(_mm_qkv_body, mode=mode, n_q=d_q // tn, n_qk=(d_q + d_kv) // tn, scale=scale)
    return pl.pallas_call(
        body,
        grid=(m // tm, n // tn),
        in_specs=[pl.BlockSpec((tm, k), lambda i, j: (i, 0)),
                  pl.BlockSpec((k, tn), lambda i, j: (0, j)),
                  pl.BlockSpec((tm, HEAD_DIM), lambda i, j: (i % s_blocks, 0)),
                  pl.BlockSpec((tm, HEAD_DIM), lambda i, j: (i % s_blocks, 0)),
                  pl.BlockSpec((1, HEAD_DIM), lambda i, j: (0, 0)),
                  pl.BlockSpec((1, HEAD_DIM), lambda i, j: (0, 0))],
        out_specs=pl.BlockSpec((tm, tn), lambda i, j: (i, j)),
        out_shape=jax.ShapeDtypeStruct((m, n), BF16),
        compiler_params=_params("parallel", "parallel"),
        name="matmul_qkv_" + mode,
    )(x, w, cos, sin, gq.reshape(1, HEAD_DIM), gk.reshape(1, HEAD_DIM))


def _rope_cos_sin(pos, dim):
    half = dim // 2
    inv = jnp.power(jnp.float32(ROPE_THETA), -jnp.arange(half, dtype=F32) * (2.0 / dim))
    ang = pos.astype(F32)[:, None] * inv[None, :]
    return jnp.cos(ang), jnp.sin(ang)


def _axial_tables(seq):
    t = jnp.arange(seq, dtype=jnp.int32)
    cr, sr = _rope_cos_sin(t // GRID_W, HEAD_DIM // 2)
    cc, sc = _rope_cos_sin(t % GRID_W, HEAD_DIM // 2)
    return (jnp.concatenate([cr, cr, cc, cc], axis=-1),
            jnp.concatenate([-sr, sr, -sc, sc], axis=-1))


def _rope_tables(seq):
    c, s = _rope_cos_sin(jnp.arange(seq, dtype=jnp.int32), HEAD_DIM)
    return jnp.concatenate([c, c], axis=-1), jnp.concatenate([-s, s], axis=-1)


def _stack_heads(q_ref):
    return jnp.concatenate([q_ref[:, g * HEAD_DIM:(g + 1) * HEAD_DIM] for g in range(GQA_GROUP)], axis=0)


def _unstack_heads(o, o_ref):
    tq = o_ref.shape[0]
    for g in range(GQA_GROUP):
        o_ref[:, g * HEAD_DIM:(g + 1) * HEAD_DIM] = o[g * tq:(g + 1) * tq].astype(o_ref.dtype)


def _qk(q, k):
    return lax.dot_general(q, k, (((1,), (1,)), ((), ())), preferred_element_type=F32)


def _attn_dense_body(q_ref, k_ref, v_ref, o_ref):
    q = _stack_heads(q_ref)
    seq = k_ref.shape[0]
    kc = min(DENSE_KEY_CHUNK, seq)
    m = l = acc = None
    for c in range(seq // kc):
        s = _qk(q, k_ref[c * kc:(c + 1) * kc, :])
        mc = jnp.max(s, axis=-1, keepdims=True)
        if c == 0:
            m = mc
            p = jnp.exp2(s - m)
            l = jnp.sum(p, axis=-1, keepdims=True)
            acc = jnp.dot(p.astype(BF16), v_ref[c * kc:(c + 1) * kc, :], preferred_element_type=F32)
        else:
            m_new = jnp.maximum(m, mc)
            alpha = jnp.exp2(m - m_new)
            p = jnp.exp2(s - m_new)
            l = alpha * l + jnp.sum(p, axis=-1, keepdims=True)
            acc = alpha * acc + jnp.dot(p.astype(BF16), v_ref[c * kc:(c + 1) * kc, :],
                                        preferred_element_type=F32)
            m = m_new
    _unstack_heads(acc / l, o_ref)


def _attn_dense(qkv, seq):
    n = qkv.shape[0]
    batch = n // seq
    n_kv = D_MODEL // (GQA_GROUP * HEAD_DIM)
    k_off = D_MODEL // HEAD_DIM
    v_off = k_off + n_kv
    tq = Q_BLOCK
    nq = seq // tq
    gw = GQA_GROUP * HEAD_DIM
    return pl.pallas_call(
        _attn_dense_body,
        grid=(batch, n_kv, nq),
        in_specs=[pl.BlockSpec((tq, gw), lambda b, h, i: (b * nq + i, h)),
                  pl.BlockSpec((seq, HEAD_DIM), lambda b, h, i: (b, k_off + h)),
                  pl.BlockSpec((seq, HEAD_DIM), lambda b, h, i: (b, v_off + h))],
        out_specs=pl.BlockSpec((tq, gw), lambda b, h, i: (b * nq + i, h)),
        out_shape=jax.ShapeDtypeStruct((n, D_MODEL), BF16),
        compiler_params=_params("parallel", "parallel", "parallel"),
        name="attn_dense",
    )(qkv, qkv, qkv)


def _attn_window_body(sink_ref, q_ref, kp_ref, kc_ref, kn_ref, vp_ref, vc_ref, vn_ref, o_ref, *, seq):
    h = pl.program_id(1)
    i = pl.program_id(2)
    tq = q_ref.shape[0]
    k3 = jnp.concatenate([kp_ref[...], kc_ref[...], kn_ref[...]], axis=0)
    v3 = jnp.concatenate([vp_ref[...], vc_ref[...], vn_ref[...]], axis=0)
    s = _qk(_stack_heads(q_ref), k3)
    row = lax.broadcasted_iota(jnp.int32, s.shape, 0)
    col = lax.broadcasted_iota(jnp.int32, s.shape, 1)
    qpos = i * tq + row % tq
    kpos = (i - 1) * tq + col
    valid = (jnp.abs(qpos - kpos) <= WINDOW) & (kpos >= 0) & (kpos < seq)
    s = jnp.where(valid, s, NEG_INF)
    rowg = lax.broadcasted_iota(jnp.int32, (s.shape[0], 1), 0) // tq
    sink = jnp.zeros((s.shape[0], 1), F32)
    for g in range(GQA_GROUP):
        sink = jnp.where(rowg == g, sink_ref[h * GQA_GROUP + g], sink)
    m = jnp.maximum(jnp.max(s, axis=-1, keepdims=True), sink)
    p = jnp.exp(s - m)
    denom = jnp.sum(p, axis=-1, keepdims=True) + jnp.exp(sink - m)
    o = jnp.dot(p.astype(BF16), v3, preferred_element_type=F32) / denom
    _unstack_heads(o, o_ref)


def _attn_window(qkv, sink, seq):
    n = qkv.shape[0]
    batch = n // seq
    n_kv = D_MODEL // (GQA_GROUP * HEAD_DIM)
    k_off = D_MODEL // HEAD_DIM
    v_off = k_off + n_kv
    tq = Q_BLOCK
    nq = seq // tq
    gw = GQA_GROUP * HEAD_DIM

    def kv_spec(off, delta):
        def index(b, h, i, sink_ref):
            return (b * nq + jnp.clip(i + delta, 0, nq - 1), off + h)
        return pl.BlockSpec((tq, HEAD_DIM), index)

    grid_spec = pltpu.PrefetchScalarGridSpec(
        num_scalar_prefetch=1,
        grid=(batch, n_kv, nq),
        in_specs=[pl.BlockSpec((tq, gw), lambda b, h, i, s_: (b * nq + i, h)),
                  kv_spec(k_off, -1), kv_spec(k_off, 0), kv_spec(k_off, 1),
                  kv_spec(v_off, -1), kv_spec(v_off, 0), kv_spec(v_off, 1)],
        out_specs=pl.BlockSpec((tq, gw), lambda b, h, i, s_: (b * nq + i, h)),
    )
    return pl.pallas_call(
        functools.partial(_attn_window_body, seq=seq),
        grid_spec=grid_spec,
        out_shape=jax.ShapeDtypeStruct((n, D_MODEL), BF16),
        compiler_params=_params("parallel", "parallel", "parallel"),
        name="attn_window",
    )(sink, qkv, qkv, qkv, qkv, qkv, qkv, qkv)


def _na_window_start(i, rows):
    return jnp.clip(2 * i - NA_ROWS // 2, 0, rows - NA_WIN_ROWS)


NA_DR = 2 * NA_ROWS - 1
NA_DC = 2 * NA_COLS - 1
NA_PAIR_LO = -2
NA_PAIRS = NA_DR + 3


def _na_table_body(rpb_ref, o_ref):
    h = pl.program_id(0)
    shape = (GRID_W, 2 * GRID_W)
    c = lax.broadcasted_iota(jnp.int32, shape, 0)
    lane = lax.broadcasted_iota(jnp.int32, shape, 1)
    kc = lane % GRID_W
    d = kc - c + (NA_COLS - 1)
    cs = jnp.clip(c - NA_COLS // 2, 0, GRID_W - NA_COLS)
    col_ok = (kc >= cs) & (kc < cs + NA_COLS)
    neg = jnp.full(shape, NEG_INF, F32)
    rows = []
    for a in range(NA_DR):
        val = jnp.zeros(shape, F32)
        for b in range(NA_DC):
            val = jnp.where(d == b, rpb_ref[(h * NA_DR + a) * NA_DC + b], val)
        rows.append(jnp.where(col_ok, val, neg))
    for e in range(NA_PAIRS):
        a0 = e + NA_PAIR_LO
        lo = rows[a0] if 0 <= a0 < NA_DR else neg
        hi = rows[a0 + 1] if 0 <= a0 + 1 < NA_DR else neg
        o_ref[0, e] = jnp.where(lane >= GRID_W, hi, lo)


def _na_table(rpb):
    n_heads = rpb.shape[0]
    grid_spec = pltpu.PrefetchScalarGridSpec(
        num_scalar_prefetch=1,
        grid=(n_heads,),
        in_specs=[],
        out_specs=pl.BlockSpec((1, NA_PAIRS, GRID_W, 2 * GRID_W), lambda h, r: (h, 0, 0, 0)),
    )
    return pl.pallas_call(
        _na_table_body,
        grid_spec=grid_spec,
        out_shape=jax.ShapeDtypeStruct((n_heads, NA_PAIRS, GRID_W, 2 * GRID_W), F32),
        compiler_params=_params("arbitrary"),
        name="na_bias_table",
    )(rpb.astype(F32).reshape(-1))


def _attn_nbr_body(q_ref, k_ref, v_ref, t_ref, o_ref, *, rows):
    i = pl.program_id(2)
    ws = _na_window_start(i, rows)
    start = pl.multiple_of(ws * GRID_W, GRID_W)
    nkeys = NA_WIN_ROWS * GRID_W
    kw = k_ref[pl.ds(start, nkeys), :]
    vw = v_ref[pl.ds(start, nkeys), :]
    lane = lax.broadcasted_iota(jnp.int32, (1, 2 * GRID_W), 1)
    blocks = [[] for _ in range(GQA_GROUP)]
    for u in range(Q_BLOCK // GRID_W):
        r = (Q_BLOCK // GRID_W) * i + u
        rs = jnp.clip(r - NA_ROWS // 2, 0, rows - NA_ROWS)
        tiles = [[] for _ in range(GQA_GROUP)]
        for jp in range(NA_WIN_ROWS // 2):
            kr = ws + 2 * jp
            ok0 = (kr >= rs) & (kr < rs + NA_ROWS)
            ok1 = (kr + 1 >= rs) & (kr + 1 < rs + NA_ROWS)
            pen = jnp.where(lane < GRID_W, jnp.where(ok0, 0.0, NEG_INF), jnp.where(ok1, 0.0, NEG_INF))
            e = kr - r + (NA_ROWS - 1) - NA_PAIR_LO
            for g in range(GQA_GROUP):
                tiles[g].append(t_ref[g, e] + pen)
        for g in range(GQA_GROUP):
            blocks[g].append(jnp.concatenate(tiles[g], axis=1))
    bias = jnp.concatenate([b for g in range(GQA_GROUP) for b in blocks[g]], axis=0)
    s = _qk(_stack_heads(q_ref), kw) + bias
    m = jnp.max(s, axis=-1, keepdims=True)
    p = jnp.exp(s - m)
    l = jnp.sum(p, axis=-1, keepdims=True)
    o = jnp.dot(p.astype(BF16), vw, preferred_element_type=F32) / l
    _unstack_heads(o, o_ref)


def _attn_nbr(qkv, rpb, seq):
    n = qkv.shape[0]
    batch = n // seq
    n_kv = D_MODEL // (GQA_GROUP * HEAD_DIM)
    k_off = D_MODEL // HEAD_DIM
    v_off = k_off + n_kv
    tq = Q_BLOCK
    nq = seq // tq
    gw = GQA_GROUP * HEAD_DIM
    rows = seq // GRID_W
    assert rows >= NA_WIN_ROWS and tq == 2 * GRID_W and rpb.shape[1:] == (NA_DR, NA_DC)
    table = _na_table(rpb)
    return pl.pallas_call(
        functools.partial(_attn_nbr_body, rows=rows),
        grid=(n_kv, batch, nq),
        in_specs=[pl.BlockSpec((tq, gw), lambda h, b, i: (b * nq + i, h)),
                  pl.BlockSpec((seq, HEAD_DIM), lambda h, b, i: (b, k_off + h)),
                  pl.BlockSpec((seq, HEAD_DIM), lambda h, b, i: (b, v_off + h)),
                  pl.BlockSpec((GQA_GROUP, NA_PAIRS, GRID_W, 2 * GRID_W), lambda h, b, i: (h, 0, 0, 0))],
        out_specs=pl.BlockSpec((tq, gw), lambda h, b, i: (b * nq + i, h)),
        out_shape=jax.ShapeDtypeStruct((n, D_MODEL), BF16),
        compiler_params=_params("parallel", "parallel", "parallel"),
        name="attn_nbr",
    )(qkv, qkv, qkv, table)


def _sgu_body(u_ref, v_ref, vg_ref, ws_ref, bs_ref, o_ref):
    v = v_ref[...]
    ms = jnp.mean(v * v, axis=-1, keepdims=True)
    vn = (v * lax.rsqrt(ms + RMS_EPS) * vg_ref[...]).astype(BF16)
    cg = v.shape[1] // SG_GROUPS
    for g in range(SG_GROUPS):
        mixed = jnp.dot(ws_ref[g], vn[:, g * cg:(g + 1) * cg], preferred_element_type=F32)
        mixed = mixed + bs_ref[:, g:g + 1]
        o_ref[:, g * cg:(g + 1) * cg] = (u_ref[:, g * cg:(g + 1) * cg] * mixed).astype(o_ref.dtype)


def _sgu(z, v_gain, w_s, b_s):
    n = z.shape[0]
    width = z.shape[1] // 2
    return pl.pallas_call(
        _sgu_body,
        grid=(n // SG_CHUNK,),
        in_specs=[pl.BlockSpec((SG_CHUNK, width), lambda c: (c, 0)),
                  pl.BlockSpec((SG_CHUNK, width), lambda c: (c, 1)),
                  pl.BlockSpec((1, width), lambda c: (0, 0)),
                  pl.BlockSpec((SG_GROUPS, SG_CHUNK, SG_CHUNK), lambda c: (0, 0, 0)),
                  pl.BlockSpec((SG_CHUNK, SG_GROUPS), lambda c: (0, 0))],
        out_specs=pl.BlockSpec((SG_CHUNK, width), lambda c: (c, 0)),
        out_shape=jax.ShapeDtypeStruct((n, width), BF16),
        compiler_params=_params("parallel"),
        name="sgu",
    )(z, z, v_gain.reshape(1, width), w_s.astype(BF16), b_s.T)


def _router_body(x_ref, g_ref, wr_ref, br_ref, xn_ref, ids_ref, gates_ref):
    x = x_ref[...]
    ms = jnp.mean(x * x, axis=-1, keepdims=True)
    xn = x * lax.rsqrt(ms + RMS_EPS) * g_ref[...]
    xn_ref[...] = xn
    logits = jnp.dot(xn, wr_ref[...], preferred_element_type=F32,
                     precision=lax.Precision.HIGHEST) + br_ref[...]
    lane = lax.broadcasted_iota(jnp.int32, logits.shape, 1)
    ninf = jnp.float32(-jnp.inf)

    lg = jnp.where(lane < N_GROUPS, logits, ninf)
    mg = jnp.max(lg, axis=-1, keepdims=True)
    grp = jnp.min(jnp.where(lg == mg, lane, LANES), axis=-1, keepdims=True)
    p_grp = 1.0 / jnp.sum(jnp.exp(lg - mg), axis=-1, keepdims=True)

    lo = N_GROUPS + grp * EXPERTS_PER_GROUP
    le = jnp.where((lane >= lo) & (lane < lo + EXPERTS_PER_GROUP), logits, ninf)
    m1 = jnp.max(le, axis=-1, keepdims=True)
    i1 = jnp.min(jnp.where(le == m1, lane, LANES), axis=-1, keepdims=True)
    le2 = jnp.where(lane == i1, ninf, le)
    m2 = jnp.max(le2, axis=-1, keepdims=True)
    i2 = jnp.min(jnp.where(le2 == m2, lane, LANES), axis=-1, keepdims=True)
    z = jnp.sum(jnp.exp(le - m1), axis=-1, keepdims=True)
    p1 = 1.0 / z
    p2 = jnp.exp(m2 - m1) / z
    tot = p1 + p2
    g1 = p_grp * (p1 / tot)
    g2 = p_grp * (p2 / tot)
    ids_ref[...] = jnp.where(lane == 0, i1 - N_GROUPS, jnp.where(lane == 1, i2 - N_GROUPS, 0))
    gates_ref[...] = jnp.where(lane == 0, g1, jnp.where(lane == 1, g2, 0.0))


def _router(h, gain, w_group, b_group, w_expert, b_expert):
    n, d = h.shape
    tm = min(256, n)
    wr = jnp.concatenate([w_group, w_expert.transpose(1, 0, 2).reshape(d, N_EXPERTS)], axis=1)
    wr = jnp.pad(wr.astype(F32), ((0, 0), (0, LANES - wr.shape[1])))
    br = jnp.concatenate([b_group, b_expert.reshape(N_EXPERTS)]).astype(F32)
    br = jnp.pad(br, (0, LANES - br.shape[0])).reshape(1, LANES)
    return pl.pallas_call(
        _router_body,
        grid=(n // tm,),
        in_specs=[pl.BlockSpec((tm, d), lambda i: (i, 0)),
                  pl.BlockSpec((1, d), lambda i: (0, 0)),
                  pl.BlockSpec((d, LANES), lambda i: (0, 0)),
                  pl.BlockSpec((1, LANES), lambda i: (0, 0))],
        out_specs=[pl.BlockSpec((tm, d), lambda i: (i, 0)),
                   pl.BlockSpec((tm, LANES), lambda i: (i, 0)),
                   pl.BlockSpec((tm, LANES), lambda i: (i, 0))],
        out_shape=[jax.ShapeDtypeStruct((n, d), F32),
                   jax.ShapeDtypeStruct((n, LANES), jnp.int32),
                   jax.ShapeDtypeStruct((n, LANES), F32)],
        compiler_params=_params("parallel"),
        name="norm_router",
    )(h, gain.reshape(1, d), wr, br)


def _dispatch_plan(ids):
    n = ids.shape[0]
    nka = n * 2
    flat_e = ids.reshape(nka)
    onehot = (flat_e[:, None] == jnp.arange(N_EXPERTS, dtype=jnp.int32)[None, :]).astype(jnp.int32)
    csum = jnp.cumsum(onehot, axis=0)
    rank = jnp.sum(csum * onehot, axis=1) - 1
    counts = csum[-1]
    padded = ((counts + MOE_BLOCK - 1) // MOE_BLOCK) * MOE_BLOCK
    pad_end = jnp.cumsum(padded)
    pad_start = pad_end - padded
    dest = (pad_start[flat_e] + rank).astype(jnp.int32)
    nb = (nka + MOE_BLOCK - 1) // MOE_BLOCK + N_EXPERTS
    flat_tok = jnp.arange(nka, dtype=jnp.int32) // 2
    row_tok = jnp.zeros((nb * MOE_BLOCK,), jnp.int32).at[dest].set(flat_tok)
    block_e = jnp.minimum(
        jnp.searchsorted(pad_end, jnp.arange(nb, dtype=jnp.int32) * MOE_BLOCK, side="right"),
        N_EXPERTS - 1).astype(jnp.int32)
    n_used = (pad_end[-1] // MOE_BLOCK).astype(jnp.int32).reshape(1)
    return dest, row_tok, block_e, n_used, nb


def _row_copy(src_hbm, row, dst, dst_row, sem):
    return pltpu.make_async_copy(src_hbm.at[pl.ds(row, 1), :], dst.at[pl.ds(dst_row, 1), :], sem)


def _experts_body(be_ref, tok_ref, nu_ref, x_hbm, w1_ref, w3_ref, w2_ref, o_ref, xbuf, sem):
    i = pl.program_id(0)
    n_used = nu_ref[0]
    rows = xbuf.shape[1]

    def start_gather(blk, slot):
        def body(r, c):
            _row_copy(x_hbm, tok_ref[blk * rows + r], xbuf.at[slot], r, sem.at[slot]).start()
            return c
        lax.fori_loop(0, rows, body, 0)

    def wait_gather(slot):
        def body(r, c):
            _row_copy(x_hbm, 0, xbuf.at[slot], r, sem.at[slot]).wait()
            return c
        lax.fori_loop(0, rows, body, 0)

    @pl.when(i < n_used)
    def _():
        slot = i % 2

        @pl.when(i == 0)
        def _():
            start_gather(0, 0)

        @pl.when(i + 1 < n_used)
        def _():
            start_gather(i + 1, 1 - slot)

        wait_gather(slot)
        x = xbuf[slot].astype(BF16)
        h1 = jnp.dot(x, w1_ref[0], preferred_element_type=F32)
        h3 = jnp.dot(x, w3_ref[0], preferred_element_type=F32)
        a = (h1 / (1.0 + jnp.exp(-h1))) * h3
        o_ref[...] = jnp.dot(a.astype(BF16), w2_ref[0], preferred_element_type=F32)

    @pl.when(i >= n_used)
    def _():
        o_ref[...] = jnp.zeros(o_ref.shape, o_ref.dtype)


def _experts(xn, row_tok, block_e, n_used, nb, w1, w3, w2):
    n, d = xn.shape
    de = w1.shape[2]
    grid_spec = pltpu.PrefetchScalarGridSpec(
        num_scalar_prefetch=3,
        grid=(nb,),
        in_specs=[pl.BlockSpec(memory_space=pl.ANY),
                  pl.BlockSpec((1, d, de), lambda i, be, tok, nu: (be[i], 0, 0)),
                  pl.BlockSpec((1, d, de), lambda i, be, tok, nu: (be[i], 0, 0)),
                  pl.BlockSpec((1, de, d), lambda i, be, tok, nu: (be[i], 0, 0))],
        out_specs=pl.BlockSpec((MOE_BLOCK, d), lambda i, be, tok, nu: (i, 0)),
        scratch_shapes=[pltpu.VMEM((2, MOE_BLOCK, d), F32),
                        pltpu.SemaphoreType.DMA((2,))],
    )
    return pl.pallas_call(
        _experts_body,
        grid_spec=grid_spec,
        out_shape=jax.ShapeDtypeStruct((nb * MOE_BLOCK, d), F32),
        compiler_params=_params("arbitrary"),
        name="experts",
    )(block_e, row_tok, n_used, xn, w1, w3, w2)


def _combine_body(pos_ref, h_ref, g_ref, y_hbm, o_ref, ybuf, sem):
    i = pl.program_id(0)
    n_steps = pl.num_programs(0)
    tm = h_ref.shape[0]

    def start_gather(step, slot):
        def body(r, c):
            base = (step * tm + r) * 2
            _row_copy(y_hbm, pos_ref[base], ybuf.at[slot, 0], r, sem.at[slot]).start()
            _row_copy(y_hbm, pos_ref[base + 1], ybuf.at[slot, 1], r, sem.at[slot]).start()
            return c
        lax.fori_loop(0, tm, body, 0)

    def wait_gather(slot):
        def body(r, c):
            _row_copy(y_hbm, 0, ybuf.at[slot, 0], r, sem.at[slot]).wait()
            _row_copy(y_hbm, 0, ybuf.at[slot, 1], r, sem.at[slot]).wait()
            return c
        lax.fori_loop(0, tm, body, 0)

    slot = i % 2

    @pl.when(i == 0)
    def _():
        start_gather(0, 0)

    @pl.when(i + 1 < n_steps)
    def _():
        start_gather(i + 1, 1 - slot)

    wait_gather(slot)
    g = g_ref[...]
    o_ref[...] = h_ref[...] + (g[:, 0:1] * ybuf[slot, 0] + g[:, 1:2] * ybuf[slot, 1])


def _combine(h, gates, pos, yb):
    n, d = h.shape
    tm = min(128, n)
    grid_spec = pltpu.PrefetchScalarGridSpec(
        num_scalar_prefetch=1,
        grid=(n // tm,),
        in_specs=[pl.BlockSpec((tm, d), lambda i, pos_: (i, 0)),
                  pl.BlockSpec((tm, LANES), lambda i, pos_: (i, 0)),
                  pl.BlockSpec(memory_space=pl.ANY)],
        out_specs=pl.BlockSpec((tm, d), lambda i, pos_: (i, 0)),
        scratch_shapes=[pltpu.VMEM((2, 2, tm, d), F32),
                        pltpu.SemaphoreType.DMA((2,))],
    )
    return pl.pallas_call(
        _combine_body,
        grid_spec=grid_spec,
        out_shape=jax.ShapeDtypeStruct((n, d), F32),
        compiler_params=_params("arbitrary"),
        name="moe_combine",
    )(pos, h, gates, yb)


def _hierarchical_moe(h, gain, w_group, b_group, w_expert, b_expert, w1, w3, w2):
    xn, ids, gates = _router(h, gain, w_group, b_group, w_expert, b_expert)
    dest, row_tok, block_e, n_used, nb = _dispatch_plan(ids[:, :2])
    yb = _experts(xn, row_tok, block_e, n_used, nb, w1.astype(BF16), w3.astype(BF16), w2.astype(BF16))
    return _combine(h, gates, dest, yb)


def kernel(x, norm_mix, norm_ffn, norm_final, a_w_in, a_q_gain, a_k_gain, a_w_o, b_w_in, b_v_gain, b_w_s, b_bias, b_w_o, c_w_in, c_sink, c_w_o, d_w_in, d_rpb, d_w_o, moe_w_group, moe_b_group, moe_w_expert, moe_b_expert, moe_w1, moe_w3, moe_w2):
    batch, seq, d = x.shape
    h = x.reshape(batch * seq, d)
    ones = jnp.ones((HEAD_DIM,), F32)
    for i in range(norm_mix.shape[0]):
        m, j = i % N_MIXERS, i // N_MIXERS
        hn = _rmsnorm(h, norm_mix[i], BF16)
        if m == 0:
            cos, sin = _axial_tables(seq)
            qkv = _matmul_qkv(hn, a_w_in[j].astype(BF16), cos, sin, a_q_gain[j], a_k_gain[j],
                              mode="axial", seq=seq)
            mixed, w_o = _attn_dense(qkv, seq), a_w_o[j]
        elif m == 1:
            z = _matmul_gelu(hn, b_w_in[j].astype(BF16), F32)
            mixed, w_o = _sgu(z, b_v_gain[j], b_w_s[j], b_bias[j]), b_w_o[j]
        elif m == 2:
            cos, sin = _rope_tables(seq)
            qkv = _matmul_qkv(hn, c_w_in[j].astype(BF16), cos, sin, ones, ones, mode="rope", seq=seq)
            mixed, w_o = _attn_window(qkv, c_sink[j].astype(F32), seq), c_w_o[j]
        else:
            cos, sin = _rope_tables(seq)
            qkv = _matmul_qkv(hn, d_w_in[j].astype(BF16), cos, sin, ones, ones, mode="none", seq=seq)
            mixed, w_o = _attn_nbr(qkv, d_rpb[j], seq), d_w_o[j]
        h = _matmul_residual(mixed, w_o.astype(BF16), h)
        h = _hierarchical_moe(h, norm_ffn[i], moe_w_group[i], moe_b_group[i], moe_w_expert[i],
                              moe_b_expert[i], moe_w1[i], moe_w3[i], moe_w2[i])
    return _rmsnorm(h, norm_final, F32).reshape(batch, seq, d)
```

```python
import functools

import jax
import jax.numpy as jnp
from jax import lax
from jax.experimental import pallas as pl
from jax.experimental.pallas import tpu as pltpu

D_MODEL = 4096
BATCH = 2
SEQ = 4096
DEPTH = 4
N_MIXERS = 4
HEAD_DIM = 128
GQA_GROUP = 4
ROPE_THETA = 10000.0
Q_BLOCK = 128
GRID_W = 64
SG_CHUNK = 128
SG_GROUPS = 8
WINDOW = 128
NA_ROWS = 8
NA_COLS = 16
N_GROUPS = 8
EXPERTS_PER_GROUP = 4
N_EXPERTS = N_GROUPS * EXPERTS_PER_GROUP
D_EXPERT = 384
MOE_ROWS = 256
RMS_EPS = 1e-6
NEG_INF = -1e30

LOG2_E = 1.4426950408889634
LANES = 128
MXU_TILE = 256
DENSE_KEY_CHUNK = 512
NA_WIN_ROWS = 10
VMEM_LIMIT = 48 * 1024 * 1024
EXPERTS_VMEM_LIMIT = 58 * 1024 * 1024

F32 = jnp.float32
BF16 = jnp.bfloat16


def _params(*sem):
    return pltpu.CompilerParams(dimension_semantics=sem, vmem_limit_bytes=VMEM_LIMIT)


def _rmsnorm_body(x_ref, g_ref, o_ref):
    x = x_ref[...]
    ms = jnp.mean(x * x, axis=-1, keepdims=True)
    o_ref[...] = (x * lax.rsqrt(ms + RMS_EPS) * g_ref[...]).astype(o_ref.dtype)


def _rmsnorm(x, gain, out_dtype):
    n, d = x.shape
    tm = min(256, n)
    return pl.pallas_call(
        _rmsnorm_body,
        grid=(n // tm,),
        in_specs=[pl.BlockSpec((tm, d), lambda i: (i, 0)),
                  pl.BlockSpec((1, d), lambda i: (0, 0))],
        out_specs=pl.BlockSpec((tm, d), lambda i: (i, 0)),
        out_shape=jax.ShapeDtypeStruct((n, d), out_dtype),
        compiler_params=_params("parallel"),
        name="rmsnorm",
    )(x, gain.reshape(1, d))


def _gelu_tanh(x):
    return 0.5 * x * (1.0 + jnp.tanh(0.7978845608028654 * (x + 0.044715 * (x * x * x))))


def _mm_gelu_body(x_ref, w_ref, o_ref):
    acc = jnp.dot(x_ref[...], w_ref[...].astype(BF16), preferred_element_type=F32)
    o_ref[...] = _gelu_tanh(acc).astype(o_ref.dtype)


def _mm_res_body(x_ref, w_ref, r_ref, o_ref):
    acc = jnp.dot(x_ref[...], w_ref[...].astype(BF16), preferred_element_type=F32)
    o_ref[...] = r_ref[...] + acc


def _swap_halves(y, width):
    if 2 * width == LANES:
        return pltpu.roll(y, width, axis=1)
    lane = lax.broadcasted_iota(jnp.int32, y.shape, 1)
    first = (lane % (2 * width)) < width
    return jnp.where(first, pltpu.roll(y, LANES - width, axis=1), pltpu.roll(y, width, axis=1))


def _mm_qkv_body(x_ref, w_ref, cos_ref, sin_ref, gq_ref, gk_ref, o_ref, *, mode, n_q, n_qk, scale):
    acc = jnp.dot(x_ref[...], w_ref[...].astype(BF16), preferred_element_type=F32)
    j = pl.program_id(1)
    heads = acc.shape[1] // HEAD_DIM
    is_q = j < n_q

    @pl.when(j < n_qk)
    def _():
        if mode == "axial":
            gain = jnp.where(is_q, gq_ref[...] * scale, gk_ref[...])
        else:
            gain = jnp.where(is_q, jnp.full((1, HEAD_DIM), scale, F32), jnp.ones((1, HEAD_DIM), F32))
        for t in range(heads):
            a = acc[:, t * HEAD_DIM:(t + 1) * HEAD_DIM]
            if mode == "axial":
                a = a * lax.rsqrt(jnp.mean(a * a, axis=-1, keepdims=True) + RMS_EPS)
            a = a * gain
            if mode == "axial":
                a = a * cos_ref[...] + _swap_halves(a, HEAD_DIM // 4) * sin_ref[...]
            elif mode == "rope":
                a = a * cos_ref[...] + _swap_halves(a, HEAD_DIM // 2) * sin_ref[...]
            o_ref[:, t * HEAD_DIM:(t + 1) * HEAD_DIM] = a.astype(o_ref.dtype)

    @pl.when(j >= n_qk)
    def _():
        o_ref[...] = acc.astype(o_ref.dtype)


def _mm_tiles(m, n):
    return min(1024, m), min(512, n)


def _matmul_gelu(x, w, out_dtype):
    m, k = x.shape
    n = w.shape[1]
    tm, tn = _mm_tiles(m, n)
    return pl.pallas_call(
        _mm_gelu_body,
        grid=(m // tm, n // tn),
        in_specs=[pl.BlockSpec((tm, k), lambda i, j: (i, 0)),
                  pl.BlockSpec((k, tn), lambda i, j: (0, j))],
        out_specs=pl.BlockSpec((tm, tn), lambda i, j: (i, j)),
        out_shape=jax.ShapeDtypeStruct((m, n), out_dtype),
        compiler_params=_params("parallel", "parallel"),
        name="matmul_gelu",
    )(x, w)


def _matmul_residual(x, w, res):
    m, k = x.shape
    n = w.shape[1]
    tm, tn = _mm_tiles(m, n)
    return pl.pallas_call(
        _mm_res_body,
        grid=(m // tm, n // tn),
        in_specs=[pl.BlockSpec((tm, k), lambda i, j: (i, 0)),
                  pl.BlockSpec((k, tn), lambda i, j: (0, j)),
                  pl.BlockSpec((tm, tn), lambda i, j: (i, j))],
        out_specs=pl.BlockSpec((tm, tn), lambda i, j: (i, j)),
        out_shape=jax.ShapeDtypeStruct((m, n), F32),
        compiler_params=_params("parallel", "parallel"),
        name="matmul_residual",
    )(x, w, res)


def _matmul_qkv(x, w, cos, sin, gq, gk, *, mode, seq):
    m, k = x.shape
    n = w.shape[1]
    d_q = D_MODEL
    d_kv = (n - d_q) // 2
    tm = min(1024, seq)
    tn = min(512, d_kv)
    s_blocks = seq // tm
    scale = HEAD_DIM ** -0.5 * (LOG2_E if mode == "axial" else 1.0)
    body = functools.partial(_mm_qkv_body, mode=mode, n_q=d_q // tn, n_qk=(d_q + d_kv) // tn, scale=scale)
    return pl.pallas_call(
        body,
        grid=(m // tm, n // tn),
        in_specs=[pl.BlockSpec((tm, k), lambda i, j: (i, 0)),
                  pl.BlockSpec((k, tn), lambda i, j: (0, j)),
                  pl.BlockSpec((tm, HEAD_DIM), lambda i, j: (i % s_blocks, 0)),
                  pl.BlockSpec((tm, HEAD_DIM), lambda i, j: (i % s_blocks, 0)),
                  pl.BlockSpec((1, HEAD_DIM), lambda i, j: (0, 0)),
                  pl.BlockSpec((1, HEAD_DIM), lambda i, j: (0, 0))],
        out_specs=pl.BlockSpec((tm, tn), lambda i, j: (i, j)),
        out_shape=jax.ShapeDtypeStruct((m, n), BF16),
        compiler_params=_params("parallel", "parallel"),
        name="matmul_qkv_" + mode,
    )(x, w, cos, sin, gq.reshape(1, HEAD_DIM), gk.reshape(1, HEAD_DIM))


def _rope_cos_sin(pos, dim):
    half = dim // 2
    inv = jnp.power(jnp.float32(ROPE_THETA), -jnp.arange(half, dtype=F32) * (2.0 / dim))
    ang = pos.astype(F32)[:, None] * inv[None, :]
    return jnp.cos(ang), jnp.sin(ang)


def _axial_tables(seq):
    t = jnp.arange(seq, dtype=jnp.int32)
    cr, sr = _rope_cos_sin(t // GRID_W, HEAD_DIM // 2)
    cc, sc = _rope_cos_sin(t % GRID_W, HEAD_DIM // 2)
    return (jnp.concatenate([cr, cr, cc, cc], axis=-1),
            jnp.concatenate([-sr, sr, -sc, sc], axis=-1))


def _rope_tables(seq):
    c, s = _rope_cos_sin(jnp.arange(seq, dtype=jnp.int32), HEAD_DIM)
    return jnp.concatenate([c, c], axis=-1), jnp.concatenate([-s, s], axis=-1)


def _stack_heads(q_ref):
    return jnp.concatenate([q_ref[:, g * HEAD_DIM:(g + 1) * HEAD_DIM] for g in range(GQA_GROUP)], axis=0)


def _unstack_heads(o, o_ref):
    tq = o_ref.shape[0]
    for g in range(GQA_GROUP):
        o_ref[:, g * HEAD_DIM:(g + 1) * HEAD_DIM] = o[g * tq:(g + 1) * tq].astype(o_ref.dtype)


def _qk(q, k):
    return lax.dot_general(q, k, (((1,), (1,)), ((), ())), preferred_element_type=F32)


def _attn_dense_body(q_ref, k_ref, v_ref, o_ref):
    q = _stack_heads(q_ref)
    seq = k_ref.shape[0]
    kc = min(DENSE_KEY_CHUNK, seq)
    m = l = acc = None
    for c in range(seq // kc):
        s = _qk(q, k_ref[c * kc:(c + 1) * kc, :])
        mc = jnp.max(s, axis=-1, keepdims=True)
        if c == 0:
            m = mc
            p = jnp.exp2(s - m)
            l = jnp.sum(p, axis=-1, keepdims=True)
            acc = jnp.dot(p.astype(BF16), v_ref[c * kc:(c + 1) * kc, :], preferred_element_type=F32)
        else:
            m_new = jnp.maximum(m, mc)
            alpha = jnp.exp2(m - m_new)
            p = jnp.exp2(s - m_new)
            l = alpha * l + jnp.sum(p, axis=-1, keepdims=True)
            acc = alpha * acc + jnp.dot(p.astype(BF16), v_ref[c * kc:(c + 1) * kc, :],
                                        preferred_element_type=F32)
            m = m_new
    _unstack_heads(acc / l, o_ref)


def _attn_dense(qkv, seq):
    n = qkv.shape[0]
    batch = n // seq
    n_kv = D_MODEL // (GQA_GROUP * HEAD_DIM)
    k_off = D_MODEL // HEAD_DIM
    v_off = k_off + n_kv
    tq = Q_BLOCK
    nq = seq // tq
    gw = GQA_GROUP * HEAD_DIM
    return pl.pallas_call(
        _attn_dense_body,
        grid=(batch, n_kv, nq),
        in_specs=[pl.BlockSpec((tq, gw), lambda b, h, i: (b * nq + i, h)),
                  pl.BlockSpec((seq, HEAD_DIM), lambda b, h, i: (b, k_off + h)),
                  pl.BlockSpec((seq, HEAD_DIM), lambda b, h, i: (b, v_off + h))],
        out_specs=pl.BlockSpec((tq, gw), lambda b, h, i: (b * nq + i, h)),
        out_shape=jax.ShapeDtypeStruct((n, D_MODEL), BF16),
        compiler_params=_params("parallel", "parallel", "parallel"),
        name="attn_dense",
    )(qkv, qkv, qkv)


def _attn_window_body(sink_ref, q_ref, kp_ref, kc_ref, kn_ref, vp_ref, vc_ref, vn_ref, o_ref, *, seq):
    h = pl.program_id(1)
    i = pl.program_id(2)
    tq = q_ref.shape[0]
    k3 = jnp.concatenate([kp_ref[...], kc_ref[...], kn_ref[...]], axis=0)
    v3 = jnp.concatenate([vp_ref[...], vc_ref[...], vn_ref[...]], axis=0)
    s = _qk(_stack_heads(q_ref), k3)
    row = lax.broadcasted_iota(jnp.int32, s.shape, 0)
    col = lax.broadcasted_iota(jnp.int32, s.shape, 1)
    qpos = i * tq + row % tq
    kpos = (i - 1) * tq + col
    valid = (jnp.abs(qpos - kpos) <= WINDOW) & (kpos >= 0) & (kpos < seq)
    s = jnp.where(valid, s, NEG_INF)
    rowg = lax.broadcasted_iota(jnp.int32, (s.shape[0], 1), 0) // tq
    sink = jnp.zeros((s.shape[0], 1), F32)
    for g in range(GQA_GROUP):
        sink = jnp.where(rowg == g, sink_ref[h * GQA_GROUP + g], sink)
    m = jnp.maximum(jnp.max(s, axis=-1, keepdims=True), sink)
    p = jnp.exp(s - m)
    denom = jnp.sum(p, axis=-1, keepdims=True) + jnp.exp(sink - m)
    o = jnp.dot(p.astype(BF16), v3, preferred_element_type=F32) / denom
    _unstack_heads(o, o_ref)


def _attn_window(qkv, sink, seq):
    n = qkv.shape[0]
    batch = n // seq
    n_kv = D_MODEL // (GQA_GROUP * HEAD_DIM)
    k_off = D_MODEL // HEAD_DIM
    v_off = k_off + n_kv
    tq = Q_BLOCK
    nq = seq // tq
    gw = GQA_GROUP * HEAD_DIM

    def kv_spec(off, delta):
        def index(b, h, i, sink_ref):
            return (b * nq + jnp.clip(i + delta, 0, nq - 1), off + h)
        return pl.BlockSpec((tq, HEAD_DIM), index)

    grid_spec = pltpu.PrefetchScalarGridSpec(
        num_scalar_prefetch=1,
        grid=(batch, n_kv, nq),
        in_specs=[pl.BlockSpec((tq, gw), lambda b, h, i, s_: (b * nq + i, h)),
                  kv_spec(k_off, -1), kv_spec(k_off, 0), kv_spec(k_off, 1),
                  kv_spec(v_off, -1), kv_spec(v_off, 0), kv_spec(v_off, 1)],
        out_specs=pl.BlockSpec((tq, gw), lambda b, h, i, s_: (b * nq + i, h)),
    )
    return pl.pallas_call(
        functools.partial(_attn_window_body, seq=seq),
        grid_spec=grid_spec,
        out_shape=jax.ShapeDtypeStruct((n, D_MODEL), BF16),
        compiler_params=_params("parallel", "parallel", "parallel"),
        name="attn_window",
    )(sink, qkv, qkv, qkv, qkv, qkv, qkv, qkv)


def _na_window_start(i, rows):
    return jnp.clip(2 * i - NA_ROWS // 2, 0, rows - NA_WIN_ROWS)


NA_DR = 2 * NA_ROWS - 1
NA_DC = 2 * NA_COLS - 1
NA_PAIR_LO = -2
NA_PAIRS = NA_DR + 3


def _na_table_body(rpb_ref, o_ref):
    h = pl.program_id(0)
    shape = (GRID_W, 2 * GRID_W)
    c = lax.broadcasted_iota(jnp.int32, shape, 0)
    lane = lax.broadcasted_iota(jnp.int32, shape, 1)
    kc = lane % GRID_W
    d = kc - c + (NA_COLS - 1)
    cs = jnp.clip(c - NA_COLS // 2, 0, GRID_W - NA_COLS)
    col_ok = (kc >= cs) & (kc < cs + NA_COLS)
    neg = jnp.full(shape, NEG_INF, F32)
    rows = []
    for a in range(NA_DR):
        val = jnp.zeros(shape, F32)
        for b in range(NA_DC):
            val = jnp.where(d == b, rpb_ref[(h * NA_DR + a) * NA_DC + b], val)
        rows.append(jnp.where(col_ok, val, neg))
    for e in range(NA_PAIRS):
        a0 = e + NA_PAIR_LO
        lo = rows[a0] if 0 <= a0 < NA_DR else neg
        hi = rows[a0 + 1] if 0 <= a0 + 1 < NA_DR else neg
        o_ref[0, e] = jnp.where(lane >= GRID_W, hi, lo)


def _na_table(rpb):
    n_heads = rpb.shape[0]
    grid_spec = pltpu.PrefetchScalarGridSpec(
        num_scalar_prefetch=1,
        grid=(n_heads,),
        in_specs=[],
        out_specs=pl.BlockSpec((1, NA_PAIRS, GRID_W, 2 * GRID_W), lambda h, r: (h, 0, 0, 0)),
    )
    return pl.pallas_call(
        _na_table_body,
        grid_spec=grid_spec,
        out_shape=jax.ShapeDtypeStruct((n_heads, NA_PAIRS, GRID_W, 2 * GRID_W), F32),
        compiler_params=_params("arbitrary"),
        name="na_bias_table",
    )(rpb.astype(F32).reshape(-1))


def _attn_nbr_body(q_ref, k_ref, v_ref, t_ref, o_ref, *, rows):
    i = pl.program_id(2)
    ws = _na_window_start(i, rows)
    start = pl.multiple_of(ws * GRID_W, GRID_W)
    nkeys = NA_WIN_ROWS * GRID_W
    kw = k_ref[pl.ds(start, nkeys), :]
    vw = v_ref[pl.ds(start, nkeys), :]
    lane = lax.broadcasted_iota(jnp.int32, (1, 2 * GRID_W), 1)
    blocks = [[] for _ in range(GQA_GROUP)]
    for u in range(Q_BLOCK // GRID_W):
        r = (Q_BLOCK // GRID_W) * i + u
        rs = jnp.clip(r - NA_ROWS // 2, 0, rows - NA_ROWS)
        tiles = [[] for _ in range(GQA_GROUP)]
        for jp in range(NA_WIN_ROWS // 2):
            kr = ws + 2 * jp
            ok0 = (kr >= rs) & (kr < rs + NA_ROWS)
            ok1 = (kr + 1 >= rs) & (kr + 1 < rs + NA_ROWS)
            pen = jnp.where(lane < GRID_W, jnp.where(ok0, 0.0, NEG_INF), jnp.where(ok1, 0.0, NEG_INF))
            e = kr - r + (NA_ROWS - 1) - NA_PAIR_LO
            for g in range(GQA_GROUP):
                tiles[g].append(t_ref[g, e] + pen)
        for g in range(GQA_GROUP):
            blocks[g].append(jnp.concatenate(tiles[g], axis=1))
    bias = jnp.concatenate([b for g in range(GQA_GROUP) for b in blocks[g]], axis=0)
    s = _qk(_stack_heads(q_ref), kw) + bias
    m = jnp.max(s, axis=-1, keepdims=True)
    p = jnp.exp(s - m)
    l = jnp.sum(p, axis=-1, keepdims=True)
    o = jnp.dot(p.astype(BF16), vw, preferred_element_type=F32) / l
    _unstack_heads(o, o_ref)


def _attn_nbr(qkv, rpb, seq):
    n = qkv.shape[0]
    batch = n // seq
    n_kv = D_MODEL // (GQA_GROUP * HEAD_DIM)
    k_off = D_MODEL // HEAD_DIM
    v_off = k_off + n_kv
    tq = Q_BLOCK
    nq = seq // tq
    gw = GQA_GROUP * HEAD_DIM
    rows = seq // GRID_W
    assert rows >= NA_WIN_ROWS and tq == 2 * GRID_W and rpb.shape[1:] == (NA_DR, NA_DC)
    table = _na_table(rpb)
    return pl.pallas_call(
        functools.partial(_attn_nbr_body, rows=rows),
        grid=(n_kv, batch, nq),
        in_specs=[pl.BlockSpec((tq, gw), lambda h, b, i: (b * nq + i, h)),
                  pl.BlockSpec((seq, HEAD_DIM), lambda h, b, i: (b, k_off + h)),
                  pl.BlockSpec((seq, HEAD_DIM), lambda h, b, i: (b, v_off + h)),
                  pl.BlockSpec((GQA_GROUP, NA_PAIRS, GRID_W, 2 * GRID_W), lambda h, b, i: (h, 0, 0, 0))],
        out_specs=pl.BlockSpec((tq, gw), lambda h, b, i: (b * nq + i, h)),
        out_shape=jax.ShapeDtypeStruct((n, D_MODEL), BF16),
        compiler_params=_params("parallel", "parallel", "parallel"),
        name="attn_nbr",
    )(qkv, qkv, qkv, table)


def _sgu_body(u_ref, v_ref, vg_ref, ws_ref, bs_ref, o_ref):
    v = v_ref[...]
    ms = jnp.mean(v * v, axis=-1, keepdims=True)
    vn = (v * lax.rsqrt(ms + RMS_EPS) * vg_ref[...]).astype(BF16)
    cg = v.shape[1] // SG_GROUPS
    for g in range(SG_GROUPS):
        mixed = jnp.dot(ws_ref[g], vn[:, g * cg:(g + 1) * cg], preferred_element_type=F32)
        mixed = mixed + bs_ref[:, g:g + 1]
        o_ref[:, g * cg:(g + 1) * cg] = (u_ref[:, g * cg:(g + 1) * cg] * mixed).astype(o_ref.dtype)


def _sgu(z, v_gain, w_s, b_s):
    n = z.shape[0]
    width = z.shape[1] // 2
    return pl.pallas_call(
        _sgu_body,
        grid=(n // SG_CHUNK,),
        in_specs=[pl.BlockSpec((SG_CHUNK, width), lambda c: (c, 0)),
                  pl.BlockSpec((SG_CHUNK, width), lambda c: (c, 1)),
                  pl.BlockSpec((1, width), lambda c: (0, 0)),
                  pl.BlockSpec((SG_GROUPS, SG_CHUNK, SG_CHUNK), lambda c: (0, 0, 0)),
                  pl.BlockSpec((SG_CHUNK, SG_GROUPS), lambda c: (0, 0))],
        out_specs=pl.BlockSpec((SG_CHUNK, width), lambda c: (c, 0)),
        out_shape=jax.ShapeDtypeStruct((n, width), BF16),
        compiler_params=_params("parallel"),
        name="sgu",
    )(z, z, v_gain.reshape(1, width), w_s.astype(BF16), b_s.T)


def _router_body(x_ref, g_ref, wr_ref, br_ref, xn_ref, ids_ref, gates_ref, counts_ref, carry):
    @pl.when(pl.program_id(0) == 0)
    def _():
        carry[...] = jnp.zeros(carry.shape, carry.dtype)

    x = x_ref[...]
    ms = jnp.mean(x * x, axis=-1, keepdims=True)
    xn = x * lax.rsqrt(ms + RMS_EPS) * g_ref[...]
    xn_ref[...] = xn
    logits = jnp.dot(xn, wr_ref[...], preferred_element_type=F32,
                     precision=lax.Precision.HIGHEST) + br_ref[...]
    lane = lax.broadcasted_iota(jnp.int32, logits.shape, 1)
    ninf = jnp.float32(-jnp.inf)

    lg = jnp.where(lane < N_GROUPS, logits, ninf)
    mg = jnp.max(lg, axis=-1, keepdims=True)
    grp = jnp.min(jnp.where(lg == mg, lane, LANES), axis=-1, keepdims=True)
    p_grp = 1.0 / jnp.sum(jnp.exp(lg - mg), axis=-1, keepdims=True)

    lo = N_GROUPS + grp * EXPERTS_PER_GROUP
    le = jnp.where((lane >= lo) & (lane < lo + EXPERTS_PER_GROUP), logits, ninf)
    m1 = jnp.max(le, axis=-1, keepdims=True)
    i1 = jnp.min(jnp.where(le == m1, lane, LANES), axis=-1, keepdims=True)
    le2 = jnp.where(lane == i1, ninf, le)
    m2 = jnp.max(le2, axis=-1, keepdims=True)
    i2 = jnp.min(jnp.where(le2 == m2, lane, LANES), axis=-1, keepdims=True)
    z = jnp.sum(jnp.exp(le - m1), axis=-1, keepdims=True)
    p1 = 1.0 / z
    p2 = jnp.exp(m2 - m1) / z
    tot = p1 + p2
    g1 = p_grp * (p1 / tot)
    g2 = p_grp * (p2 / tot)
    gates_ref[...] = jnp.where(lane == 0, g1, jnp.where(lane == 1, g2, 0.0))

    e0 = i1 - N_GROUPS
    e1 = i2 - N_GROUPS
    tm = x.shape[0]
    onehot = jnp.where((lane == e0) | (lane == e1), 1.0, 0.0)
    tri = jnp.where(lax.broadcasted_iota(jnp.int32, (tm, tm), 0) > lax.broadcasted_iota(jnp.int32, (tm, tm), 1),
                    1.0, 0.0).astype(BF16)
    before = jnp.dot(tri, onehot.astype(BF16), preferred_element_type=F32) + carry[...]
    r0 = jnp.sum(jnp.where(lane == e0, before, 0.0), axis=-1, keepdims=True).astype(jnp.int32)
    r1 = jnp.sum(jnp.where(lane == e1, before, 0.0), axis=-1, keepdims=True).astype(jnp.int32)
    carry[...] = carry[...] + jnp.sum(onehot, axis=0, keepdims=True)
    counts_ref[...] = carry[...].astype(jnp.int32)
    ids_ref[...] = jnp.where(lane == 0, e0, jnp.where(lane == 1, e1, jnp.where(lane == 2, r0, jnp.where(lane == 3, r1, 0))))


def _router(h, gain, w_group, b_group, w_expert, b_expert):
    n, d = h.shape
    tm = min(256, n)
    wr = jnp.concatenate([w_group, w_expert.transpose(1, 0, 2).reshape(d, N_EXPERTS)], axis=1)
    wr = jnp.pad(wr.astype(F32), ((0, 0), (0, LANES - wr.shape[1])))
    br = jnp.concatenate([b_group, b_expert.reshape(N_EXPERTS)]).astype(F32)
    br = jnp.pad(br, (0, LANES - br.shape[0])).reshape(1, LANES)
    return pl.pallas_call(
        _router_body,
        grid=(n // tm,),
        in_specs=[pl.BlockSpec((tm, d), lambda i: (i, 0)),
                  pl.BlockSpec((1, d), lambda i: (0, 0)),
                  pl.BlockSpec((d, LANES), lambda i: (0, 0)),
                  pl.BlockSpec((1, LANES), lambda i: (0, 0))],
        out_specs=[pl.BlockSpec((tm, d), lambda i: (i, 0)),
                   pl.BlockSpec((tm, LANES), lambda i: (i, 0)),
                   pl.BlockSpec((tm, LANES), lambda i: (i, 0)),
                   pl.BlockSpec((1, LANES), lambda i: (0, 0))],
        out_shape=[jax.ShapeDtypeStruct((n, d), F32),
                   jax.ShapeDtypeStruct((n, LANES), jnp.int32),
                   jax.ShapeDtypeStruct((n, LANES), F32),
                   jax.ShapeDtypeStruct((1, LANES), jnp.int32)],
        scratch_shapes=[pltpu.VMEM((1, LANES), F32)],
        compiler_params=_params("arbitrary"),
        name="norm_router",
    )(h, gain.reshape(1, d), wr, br)


def _dispatch_plan(ids, counts):
    n = ids.shape[0]
    nka = n * 2
    nb = (nka + MOE_ROWS - 1) // MOE_ROWS + N_EXPERTS + 1
    e, rank = ids[:, :2], ids[:, 2:4]
    blocks = (counts + MOE_ROWS - 1) // MOE_ROWS
    blk_end = jnp.cumsum(blocks)
    blk_start = blk_end - blocks
    n_used = blk_end[-1]
    lanes = jnp.arange(N_EXPERTS, dtype=jnp.int32)
    row_start = jnp.sum(jnp.where(e[:, :, None] == lanes, blk_start * MOE_ROWS, 0), axis=-1)
    dest = (row_start + rank).astype(jnp.int32).reshape(nka)
    flat_tok = jnp.arange(nka, dtype=jnp.int32) // 2
    row_tok = jnp.zeros((nb * MOE_ROWS,), jnp.int32).at[dest].set(flat_tok)
    b = jnp.arange(nb, dtype=jnp.int32)
    block_e = jnp.minimum(jnp.sum(b[:, None] >= blk_end[None, :], axis=1), N_EXPERTS - 1).astype(jnp.int32)
    first = ((b == blk_start[block_e]) & (b < n_used)).astype(jnp.int32)
    nxt_blk = blk_end[block_e]
    nxt_e = jnp.where(nxt_blk < n_used, block_e[jnp.minimum(nxt_blk, nb - 1)], -1).astype(jnp.int32)
    return dest, row_tok, block_e, first, nxt_e, n_used.astype(jnp.int32).reshape(1), nb


def _row_copy(src_hbm, row, dst, dst_row, sem):
    return pltpu.make_async_copy(src_hbm.at[pl.ds(row, 1), :], dst.at[pl.ds(dst_row, 1), :], sem)


def _experts_body(be_ref, first_ref, nxt_ref, tok_ref, nu_ref, x_hbm, w1_hbm, w3_hbm, w2_hbm, o_ref,
                  xbuf, st13, st2, wb13, wb2, gsem, wsem, *, layer):
    i = pl.program_id(0)
    n_used = nu_ref[0]
    rows = xbuf.shape[1]
    de = st2.shape[0]
    slot = i % 2

    def weight_copies(e):
        return (pltpu.make_async_copy(w1_hbm.at[layer, e], st13.at[0], wsem.at[0]),
                pltpu.make_async_copy(w3_hbm.at[layer, e], st13.at[1], wsem.at[1]),
                pltpu.make_async_copy(w2_hbm.at[layer, e], st2, wsem.at[2]))

    def start_gather(blk, dst_slot, lo=0, hi=rows):
        for r in range(lo, hi):
            _row_copy(x_hbm, tok_ref[blk * rows + r], xbuf.at[dst_slot], r, gsem.at[dst_slot]).start()

    @pl.when(i == 0)
    def _():
        for c in weight_copies(be_ref[0]):
            c.start()
        start_gather(0, 0)

    @pl.when(i <= n_used)
    def _():
        for r in range(rows):
            _row_copy(x_hbm, 0, xbuf.at[slot], r, gsem.at[slot]).wait()

    @pl.when(i < n_used)
    def _():
        @pl.when(first_ref[i] == 1)
        def _():
            for c in weight_copies(0):
                c.wait()
            wb13[:, :de] = st13[0].astype(BF16)
            wb13[:, de:] = st13[1].astype(BF16)
            wb2[...] = st2[...].astype(BF16)

            @pl.when(nxt_ref[i] >= 0)
            def _():
                for c in weight_copies(nxt_ref[i]):
                    c.start()

        pieces = 8
        per = rows // pieces
        x = xbuf[slot].astype(BF16)
        half = rows // 2
        wide = (2 * de) // (2 * MXU_TILE) * (2 * MXU_TILE)
        start_gather(i + 1, 1 - slot, 0, per)
        h_a = jnp.dot(x, wb13[:, :wide], preferred_element_type=F32)
        start_gather(i + 1, 1 - slot, per, 2 * per)
        h_b = jnp.concatenate(
            [jnp.dot(x[:half], wb13[:, wide:], preferred_element_type=F32),
             jnp.dot(x[half:], wb13[:, wide:], preferred_element_type=F32)], axis=0)
        h13 = jnp.concatenate([h_a, h_b], axis=1)
        start_gather(i + 1, 1 - slot, 2 * per, 4 * per)
        h1, h3 = h13[:, :de], h13[:, de:]
        a = ((h1 / (1.0 + jnp.exp(-h1))) * h3).astype(BF16)
        d = o_ref.shape[1]
        cols = d // 4
        for c in range(4):
            o_ref[:, c * cols:(c + 1) * cols] = jnp.dot(a, wb2[:, c * cols:(c + 1) * cols],
                                                        preferred_element_type=F32)
            start_gather(i + 1, 1 - slot, (4 + c) * per, (5 + c) * per)

    @pl.when(i >= n_used)
    def _():
        o_ref[...] = jnp.zeros(o_ref.shape, o_ref.dtype)


def _experts(xn, row_tok, block_e, first, nxt_e, n_used, nb, w1, w3, w2, *, layer):
    n, d = xn.shape
    de = w1.shape[3]
    grid_spec = pltpu.PrefetchScalarGridSpec(
        num_scalar_prefetch=5,
        grid=(nb,),
        in_specs=[pl.BlockSpec(memory_space=pl.ANY)] * 4,
        out_specs=pl.BlockSpec((MOE_ROWS, d), lambda i, *_: (i, 0)),
        scratch_shapes=[pltpu.VMEM((2, MOE_ROWS, d), F32),
                        pltpu.VMEM((2, d, de), F32),
                        pltpu.VMEM((de, d), F32),
                        pltpu.VMEM((d, 2 * de), BF16),
                        pltpu.VMEM((de, d), BF16),
                        pltpu.SemaphoreType.DMA((2,)),
                        pltpu.SemaphoreType.DMA((3,))],
    )
    return pl.pallas_call(
        functools.partial(_experts_body, layer=layer),
        grid_spec=grid_spec,
        out_shape=jax.ShapeDtypeStruct((nb * MOE_ROWS, d), F32),
        compiler_params=pltpu.CompilerParams(dimension_semantics=("arbitrary",),
                                             vmem_limit_bytes=EXPERTS_VMEM_LIMIT),
        name="experts",
    )(block_e, first, nxt_e, row_tok, n_used, xn, w1, w3, w2)


def _combine_body(pos_ref, h_ref, g_ref, y_hbm, o_ref, ybuf, sem):
    i = pl.program_id(0)
    n_steps = pl.num_programs(0)
    tm = h_ref.shape[0]

    def start_gather(step, slot):
        for r in range(tm):
            base = (step * tm + r) * 2
            _row_copy(y_hbm, pos_ref[base], ybuf.at[slot, 0], r, sem.at[slot]).start()
            _row_copy(y_hbm, pos_ref[base + 1], ybuf.at[slot, 1], r, sem.at[slot]).start()

    def wait_gather(slot):
        for r in range(tm):
            _row_copy(y_hbm, 0, ybuf.at[slot, 0], r, sem.at[slot]).wait()
            _row_copy(y_hbm, 0, ybuf.at[slot, 1], r, sem.at[slot]).wait()

    slot = i % 2

    @pl.when(i == 0)
    def _():
        start_gather(0, 0)

    @pl.when(i + 1 < n_steps)
    def _():
        start_gather(i + 1, 1 - slot)

    wait_gather(slot)
    g = g_ref[...]
    o_ref[...] = h_ref[...] + (g[:, 0:1] * ybuf[slot, 0] + g[:, 1:2] * ybuf[slot, 1])


def _combine(h, gates, pos, yb):
    n, d = h.shape
    tm = min(128, n)
    grid_spec = pltpu.PrefetchScalarGridSpec(
        num_scalar_prefetch=1,
        grid=(n // tm,),
        in_specs=[pl.BlockSpec((tm, d), lambda i, pos_: (i, 0)),
                  pl.BlockSpec((tm, LANES), lambda i, pos_: (i, 0)),
                  pl.BlockSpec(memory_space=pl.ANY)],
        out_specs=pl.BlockSpec((tm, d), lambda i, pos_: (i, 0)),
        scratch_shapes=[pltpu.VMEM((2, 2, tm, d), F32),
                        pltpu.SemaphoreType.DMA((2,))],
    )
    return pl.pallas_call(
        _combine_body,
        grid_spec=grid_spec,
        out_shape=jax.ShapeDtypeStruct((n, d), F32),
        compiler_params=_params("arbitrary"),
        name="moe_combine",
    )(pos, h, gates, yb)


def _hierarchical_moe(h, gain, w_group, b_group, w_expert, b_expert, w1, w3, w2, *, layer):
    xn, ids, gates, counts = _router(h, gain, w_group, b_group, w_expert, b_expert)
    dest, row_tok, block_e, first, nxt_e, n_used, nb = _dispatch_plan(ids[:, :4], counts[0, :N_EXPERTS])
    yb = _experts(xn, row_tok, block_e, first, nxt_e, n_used, nb, w1, w3, w2, layer=layer)
    return _combine(h, gates, dest, yb)


def kernel(x, norm_mix, norm_ffn, norm_final, a_w_in, a_q_gain, a_k_gain, a_w_o, b_w_in, b_v_gain, b_w_s, b_bias, b_w_o, c_w_in, c_sink, c_w_o, d_w_in, d_rpb, d_w_o, moe_w_group, moe_b_group, moe_w_expert, moe_b_expert, moe_w1, moe_w3, moe_w2):
    batch, seq, d = x.shape
    h = x.reshape(batch * seq, d)
    ones = jnp.ones((HEAD_DIM,), F32)
    for i in range(norm_mix.shape[0]):
        m, j = i % N_MIXERS, i // N_MIXERS
        hn = _rmsnorm(h, norm_mix[i], BF16)
        if m == 0:
            cos, sin = _axial_tables(seq)
            qkv = _matmul_qkv(hn, a_w_in[j], cos, sin, a_q_gain[j], a_k_gain[j],
                              mode="axial", seq=seq)
            mixed, w_o = _attn_dense(qkv, seq), a_w_o[j]
        elif m == 1:
            z = _matmul_gelu(hn, b_w_in[j], F32)
            mixed, w_o = _sgu(z, b_v_gain[j], b_w_s[j], b_bias[j]), b_w_o[j]
        elif m == 2:
            cos, sin = _rope_tables(seq)
            qkv = _matmul_qkv(hn, c_w_in[j], cos, sin, ones, ones, mode="rope", seq=seq)
            mixed, w_o = _attn_window(qkv, c_sink[j].astype(F32), seq), c_w_o[j]
        else:
            cos, sin = _rope_tables(seq)
            qkv = _matmul_qkv(hn, d_w_in[j], cos, sin, ones, ones, mode="none", seq=seq)
            mixed, w_o = _attn_nbr(qkv, d_rpb[j], seq), d_w_o[j]
        h = _matmul_residual(mixed, w_o, h)
        h = _hierarchical_moe(h, norm_ffn[i], moe_w_group[i], moe_b_group[i], moe_w_expert[i],
                              moe_b_expert[i], moe_w1, moe_w3, moe_w2, layer=i)
    return _rmsnorm(h, norm_final, F32).reshape(batch, seq, d)
```

```python
import functools

import jax
import jax.numpy as jnp
from jax import lax
from jax.experimental import pallas as pl
from jax.experimental.pallas import tpu as pltpu

D_MODEL = 4096
BATCH = 2
SEQ = 4096
DEPTH = 4
N_MIXERS = 4
HEAD_DIM = 128
GQA_GROUP = 4
ROPE_THETA = 10000.0
Q_BLOCK = 128
GRID_W = 64
SG_CHUNK = 128
SG_GROUPS = 8
WINDOW = 128
NA_ROWS = 8
NA_COLS = 16
N_GROUPS = 8
EXPERTS_PER_GROUP = 4
N_EXPERTS = N_GROUPS * EXPERTS_PER_GROUP
D_EXPERT = 384
MOE_ROWS = 256
RMS_EPS = 1e-6
NEG_INF = -1e30

LOG2_E = 1.4426950408889634
LANES = 128
MXU_TILE = 256
WEIGHT_DMA_PRIORITY = 1
DENSE_KEY_CHUNK = 1024
NA_WIN_ROWS = 10
VMEM_LIMIT = 48 * 1024 * 1024
EXPERTS_VMEM_LIMIT = 58 * 1024 * 1024

F32 = jnp.float32
BF16 = jnp.bfloat16
PACKED = jnp.uint32


def _params(*sem):
    return pltpu.CompilerParams(dimension_semantics=sem, vmem_limit_bytes=VMEM_LIMIT)


def _rmsnorm_body(x_ref, g_ref, o_ref):
    x = x_ref[...]
    ms = jnp.mean(x * x, axis=-1, keepdims=True)
    o_ref[...] = (x * lax.rsqrt(ms + RMS_EPS) * g_ref[...]).astype(o_ref.dtype)


def _rmsnorm(x, gain, out_dtype):
    n, d = x.shape
    tm = min(256, n)
    return pl.pallas_call(
        _rmsnorm_body,
        grid=(n // tm,),
        in_specs=[pl.BlockSpec((tm, d), lambda i: (i, 0)),
                  pl.BlockSpec((1, d), lambda i: (0, 0))],
        out_specs=pl.BlockSpec((tm, d), lambda i: (i, 0)),
        out_shape=jax.ShapeDtypeStruct((n, d), out_dtype),
        compiler_params=_params("parallel"),
        name="rmsnorm",
    )(x, gain.reshape(1, d))


def _gelu_tanh(x):
    return 0.5 * x * (1.0 + jnp.tanh(0.7978845608028654 * (x + 0.044715 * (x * x * x))))


def _mm_gelu_body(x_ref, w_ref, o_ref):
    acc = jnp.dot(x_ref[...], w_ref[...].astype(BF16), preferred_element_type=F32)
    o_ref[...] = _gelu_tanh(acc).astype(o_ref.dtype)


def _mm_res_body(x_ref, w_ref, r_ref, o_ref):
    acc = jnp.dot(x_ref[...], w_ref[...].astype(BF16), preferred_element_type=F32)
    o_ref[...] = r_ref[...] + acc


def _swap_halves(y, width):
    if 2 * width == LANES:
        return pltpu.roll(y, width, axis=1)
    lane = lax.broadcasted_iota(jnp.int32, y.shape, 1)
    first = (lane % (2 * width)) < width
    return jnp.where(first, pltpu.roll(y, LANES - width, axis=1), pltpu.roll(y, width, axis=1))


def _mm_qkv_body(x_ref, w_ref, cos_ref, sin_ref, gq_ref, gk_ref, o_ref, *, mode, n_q, n_qk, scale):
    acc = jnp.dot(x_ref[...], w_ref[...].astype(BF16), preferred_element_type=F32)
    j = pl.program_id(1)
    heads = acc.shape[1] // HEAD_DIM
    is_q = j < n_q

    @pl.when(j < n_qk)
    def _():
        if mode == "axial":
            gain = jnp.where(is_q, gq_ref[...] * scale, gk_ref[...])
        else:
            gain = jnp.where(is_q, jnp.full((1, HEAD_DIM), scale, F32), jnp.ones((1, HEAD_DIM), F32))
        for t in range(heads):
            a = acc[:, t * HEAD_DIM:(t + 1) * HEAD_DIM]
            if mode == "axial":
                a = a * lax.rsqrt(jnp.mean(a * a, axis=-1, keepdims=True) + RMS_EPS)
            a = a * gain
            if mode == "axial":
                a = a * cos_ref[...] + _swap_halves(a, HEAD_DIM // 4) * sin_ref[...]
            elif mode == "rope":
                a = a * cos_ref[...] + _swap_halves(a, HEAD_DIM // 2) * sin_ref[...]
            o_ref[:, t * HEAD_DIM:(t + 1) * HEAD_DIM] = a.astype(o_ref.dtype)

    @pl.when(j >= n_qk)
    def _():
        o_ref[...] = acc.astype(o_ref.dtype)


def _mm_tiles(m, n):
    return min(1024, m), min(512, n)


def _matmul_gelu(x, w, out_dtype):
    m, k = x.shape
    n = w.shape[1]
    tm, tn = _mm_tiles(m, n)
    return pl.pallas_call(
        _mm_gelu_body,
        grid=(m // tm, n // tn),
        in_specs=[pl.BlockSpec((tm, k), lambda i, j: (i, 0)),
                  pl.BlockSpec((k, tn), lambda i, j: (0, j))],
        out_specs=pl.BlockSpec((tm, tn), lambda i, j: (i, j)),
        out_shape=jax.ShapeDtypeStruct((m, n), out_dtype),
        compiler_params=_params("parallel", "parallel"),
        name="matmul_gelu",
    )(x, w)


def _matmul_residual(x, w, res):
    m, k = x.shape
    n = w.shape[1]
    tm, tn = _mm_tiles(m, n)
    return pl.pallas_call(
        _mm_res_body,
        grid=(m // tm, n // tn),
        in_specs=[pl.BlockSpec((tm, k), lambda i, j: (i, 0)),
                  pl.BlockSpec((k, tn), lambda i, j: (0, j)),
                  pl.BlockSpec((tm, tn), lambda i, j: (i, j))],
        out_specs=pl.BlockSpec((tm, tn), lambda i, j: (i, j)),
        out_shape=jax.ShapeDtypeStruct((m, n), F32),
        compiler_params=_params("parallel", "parallel"),
        name="matmul_residual",
    )(x, w, res)


def _matmul_qkv(x, w, cos, sin, gq, gk, *, mode, seq):
    m, k = x.shape
    n = w.shape[1]
    d_q = D_MODEL
    d_kv = (n - d_q) // 2
    tm = min(1024, seq)
    tn = min(512, d_kv)
    s_blocks = seq // tm
    scale = HEAD_DIM ** -0.5 * (LOG2_E if mode == "axial" else 1.0)
    body = functools.partial(_mm_qkv_body, mode=mode, n_q=d_q // tn, n_qk=(d_q + d_kv) // tn, scale=scale)
    return pl.pallas_call(
        body,
        grid=(m // tm, n // tn),
        in_specs=[pl.BlockSpec((tm, k), lambda i, j: (i, 0)),
                  pl.BlockSpec((k, tn), lambda i, j: (0, j)),
                  pl.BlockSpec((tm, HEAD_DIM), lambda i, j: (i % s_blocks, 0)),
                  pl.BlockSpec((tm, HEAD_DIM), lambda i, j: (i % s_blocks, 0)),
                  pl.BlockSpec((1, HEAD_DIM), lambda i, j: (0, 0)),
                  pl.BlockSpec((1, HEAD_DIM), lambda i, j: (0, 0))],
        out_specs=pl.BlockSpec((tm, tn), lambda i, j: (i, j)),
        out_shape=jax.ShapeDtypeStruct((m, n), BF16),
        compiler_params=_params("parallel", "parallel"),
        name="matmul_qkv_" + mode,
    )(x, w, cos, sin, gq.reshape(1, HEAD_DIM), gk.reshape(1, HEAD_DIM))


def _rope_cos_sin(pos, dim):
    half = dim // 2
    inv = jnp.power(jnp.float32(ROPE_THETA), -jnp.arange(half, dtype=F32) * (2.0 / dim))
    ang = pos.astype(F32)[:, None] * inv[None, :]
    return jnp.cos(ang), jnp.sin(ang)


def _axial_tables(seq):
    t = jnp.arange(seq, dtype=jnp.int32)
    cr, sr = _rope_cos_sin(t // GRID_W, HEAD_DIM // 2)
    cc, sc = _rope_cos_sin(t % GRID_W, HEAD_DIM // 2)
    return (jnp.concatenate([cr, cr, cc, cc], axis=-1),
            jnp.concatenate([-sr, sr, -sc, sc], axis=-1))


def _rope_tables(seq):
    c, s = _rope_cos_sin(jnp.arange(seq, dtype=jnp.int32), HEAD_DIM)
    return jnp.concatenate([c, c], axis=-1), jnp.concatenate([-s, s], axis=-1)


def _stack_heads(q_ref):
    return jnp.concatenate([q_ref[:, g * HEAD_DIM:(g + 1) * HEAD_DIM] for g in range(GQA_GROUP)], axis=0)


def _unstack_heads(o, o_ref):
    tq = o_ref.shape[0]
    for g in range(GQA_GROUP):
        o_ref[:, g * HEAD_DIM:(g + 1) * HEAD_DIM] = o[g * tq:(g + 1) * tq].astype(o_ref.dtype)


def _qk(q, k):
    return lax.dot_general(q, k, (((1,), (1,)), ((), ())), preferred_element_type=F32)


def _attn_dense_body(q_ref, k_ref, v_ref, o_ref):
    q = _stack_heads(q_ref)
    seq = k_ref.shape[0]
    kc = min(DENSE_KEY_CHUNK, seq)
    m = l = acc = None
    for c in range(seq // kc):
        s = _qk(q, k_ref[c * kc:(c + 1) * kc, :])
        mc = jnp.max(s, axis=-1, keepdims=True)
        if c == 0:
            m = mc
            p = jnp.exp2(s - m)
            l = jnp.sum(p, axis=-1, keepdims=True)
            acc = jnp.dot(p.astype(BF16), v_ref[c * kc:(c + 1) * kc, :], preferred_element_type=F32)
        else:
            m_new = jnp.maximum(m, mc)
            alpha = jnp.exp2(m - m_new)
            p = jnp.exp2(s - m_new)
            l = alpha * l + jnp.sum(p, axis=-1, keepdims=True)
            acc = alpha * acc + jnp.dot(p.astype(BF16), v_ref[c * kc:(c + 1) * kc, :],
                                        preferred_element_type=F32)
            m = m_new
    _unstack_heads(acc / l, o_ref)


def _attn_dense(qkv, seq):
    n = qkv.shape[0]
    batch = n // seq
    n_kv = D_MODEL // (GQA_GROUP * HEAD_DIM)
    k_off = D_MODEL // HEAD_DIM
    v_off = k_off + n_kv
    tq = Q_BLOCK
    nq = seq // tq
    gw = GQA_GROUP * HEAD_DIM
    return pl.pallas_call(
        _attn_dense_body,
        grid=(batch, n_kv, nq),
        in_specs=[pl.BlockSpec((tq, gw), lambda b, h, i: (b * nq + i, h)),
                  pl.BlockSpec((seq, HEAD_DIM), lambda b, h, i: (b, k_off + h)),
                  pl.BlockSpec((seq, HEAD_DIM), lambda b, h, i: (b, v_off + h))],
        out_specs=pl.BlockSpec((tq, gw), lambda b, h, i: (b * nq + i, h)),
        out_shape=jax.ShapeDtypeStruct((n, D_MODEL), BF16),
        compiler_params=_params("parallel", "parallel", "parallel"),
        name="attn_dense",
    )(qkv, qkv, qkv)


def _attn_window_body(sink_ref, q_ref, k_ref, v_ref, o_ref):
    h = pl.program_id(1)
    i = pl.program_id(2)
    tq = q_ref.shape[0]
    nkeys = tq + 2 * WINDOW
    start = pl.multiple_of(jnp.clip(i * tq - WINDOW, 0, k_ref.shape[0] - nkeys), WINDOW)
    k3 = k_ref[pl.ds(start, nkeys), :]
    v3 = v_ref[pl.ds(start, nkeys), :]
    s = _qk(_stack_heads(q_ref), k3)
    row = lax.broadcasted_iota(jnp.int32, s.shape, 0)
    col = lax.broadcasted_iota(jnp.int32, s.shape, 1)
    qpos = i * tq + row % tq
    kpos = start + col
    s = jnp.where(jnp.abs(qpos - kpos) <= WINDOW, s, NEG_INF)
    rowg = lax.broadcasted_iota(jnp.int32, (s.shape[0], 1), 0) // tq
    sink = jnp.zeros((s.shape[0], 1), F32)
    for g in range(GQA_GROUP):
        sink = jnp.where(rowg == g, sink_ref[h * GQA_GROUP + g], sink)
    m = jnp.maximum(jnp.max(s, axis=-1, keepdims=True), sink)
    p = jnp.exp(s - m)
    denom = jnp.sum(p, axis=-1, keepdims=True) + jnp.exp(sink - m)
    o = jnp.dot(p.astype(BF16), v3, preferred_element_type=F32) / denom
    _unstack_heads(o, o_ref)


def _attn_window(qkv, sink, seq):
    n = qkv.shape[0]
    batch = n // seq
    n_kv = D_MODEL // (GQA_GROUP * HEAD_DIM)
    k_off = D_MODEL // HEAD_DIM
    v_off = k_off + n_kv
    tq = 2 * Q_BLOCK
    nq = seq // tq
    gw = GQA_GROUP * HEAD_DIM
    assert seq >= tq + 2 * WINDOW
    grid_spec = pltpu.PrefetchScalarGridSpec(
        num_scalar_prefetch=1,
        grid=(batch, n_kv, nq),
        in_specs=[pl.BlockSpec((tq, gw), lambda b, h, i, s_: (b * nq + i, h)),
                  pl.BlockSpec((seq, HEAD_DIM), lambda b, h, i, s_: (b, k_off + h)),
                  pl.BlockSpec((seq, HEAD_DIM), lambda b, h, i, s_: (b, v_off + h))],
        out_specs=pl.BlockSpec((tq, gw), lambda b, h, i, s_: (b * nq + i, h)),
    )
    return pl.pallas_call(
        _attn_window_body,
        grid_spec=grid_spec,
        out_shape=jax.ShapeDtypeStruct((n, D_MODEL), BF16),
        compiler_params=_params("parallel", "parallel", "parallel"),
        name="attn_window",
    )(sink, qkv, qkv, qkv)


def _na_window_start(i, rows):
    return jnp.clip(2 * i - NA_ROWS // 2, 0, rows - NA_WIN_ROWS)


NA_DR = 2 * NA_ROWS - 1
NA_DC = 2 * NA_COLS - 1
NA_PAIR_LO = -2
NA_PAIRS = NA_DR + 3


def _na_table_body(rpb_ref, o_ref):
    h = pl.program_id(0)
    shape = (GRID_W, 2 * GRID_W)
    c = lax.broadcasted_iota(jnp.int32, shape, 0)
    lane = lax.broadcasted_iota(jnp.int32, shape, 1)
    kc = lane % GRID_W
    d = kc - c + (NA_COLS - 1)
    cs = jnp.clip(c - NA_COLS // 2, 0, GRID_W - NA_COLS)
    col_ok = (kc >= cs) & (kc < cs + NA_COLS)
    neg = jnp.full(shape, NEG_INF, F32)
    rows = []
    for a in range(NA_DR):
        val = jnp.zeros(shape, F32)
        for b in range(NA_DC):
            val = jnp.where(d == b, rpb_ref[(h * NA_DR + a) * NA_DC + b], val)
        rows.append(jnp.where(col_ok, val, neg))
    for e in range(NA_PAIRS):
        a0 = e + NA_PAIR_LO
        lo = rows[a0] if 0 <= a0 < NA_DR else neg
        hi = rows[a0 + 1] if 0 <= a0 + 1 < NA_DR else neg
        o_ref[0, e] = jnp.where(lane >= GRID_W, hi, lo)


def _na_table(rpb):
    n_heads = rpb.shape[0]
    grid_spec = pltpu.PrefetchScalarGridSpec(
        num_scalar_prefetch=1,
        grid=(n_heads,),
        in_specs=[],
        out_specs=pl.BlockSpec((1, NA_PAIRS, GRID_W, 2 * GRID_W), lambda h, r: (h, 0, 0, 0)),
    )
    return pl.pallas_call(
        _na_table_body,
        grid_spec=grid_spec,
        out_shape=jax.ShapeDtypeStruct((n_heads, NA_PAIRS, GRID_W, 2 * GRID_W), F32),
        compiler_params=_params("arbitrary"),
        name="na_bias_table",
    )(rpb.astype(F32).reshape(-1))


def _attn_nbr_body(q_ref, k_ref, v_ref, t_ref, o_ref, *, rows):
    i = pl.program_id(2)
    ws = _na_window_start(i, rows)
    start = pl.multiple_of(ws * GRID_W, GRID_W)
    nkeys = NA_WIN_ROWS * GRID_W
    kw = k_ref[pl.ds(start, nkeys), :]
    vw = v_ref[pl.ds(start, nkeys), :]
    lane = lax.broadcasted_iota(jnp.int32, (1, 2 * GRID_W), 1)
    blocks = [[] for _ in range(GQA_GROUP)]
    for u in range(Q_BLOCK // GRID_W):
        r = (Q_BLOCK // GRID_W) * i + u
        rs = jnp.clip(r - NA_ROWS // 2, 0, rows - NA_ROWS)
        tiles = [[] for _ in range(GQA_GROUP)]
        for jp in range(NA_WIN_ROWS // 2):
            kr = ws + 2 * jp
            ok0 = (kr >= rs) & (kr < rs + NA_ROWS)
            ok1 = (kr + 1 >= rs) & (kr + 1 < rs + NA_ROWS)
            pen = jnp.where(lane < GRID_W, jnp.where(ok0, 0.0, NEG_INF), jnp.where(ok1, 0.0, NEG_INF))
            e = kr - r + (NA_ROWS - 1) - NA_PAIR_LO
            for g in range(GQA_GROUP):
                tiles[g].append(t_ref[g, e] + pen)
        for g in range(GQA_GROUP):
            blocks[g].append(jnp.concatenate(tiles[g], axis=1))
    bias = jnp.concatenate([b for g in range(GQA_GROUP) for b in blocks[g]], axis=0)
    s = _qk(_stack_heads(q_ref), kw) + bias
    m = jnp.max(s, axis=-1, keepdims=True)
    p = jnp.exp(s - m)
    l = jnp.sum(p, axis=-1, keepdims=True)
    o = jnp.dot(p.astype(BF16), vw, preferred_element_type=F32) / l
    _unstack_heads(o, o_ref)


def _attn_nbr(qkv, rpb, seq):
    n = qkv.shape[0]
    batch = n // seq
    n_kv = D_MODEL // (GQA_GROUP * HEAD_DIM)
    k_off = D_MODEL // HEAD_DIM
    v_off = k_off + n_kv
    tq = Q_BLOCK
    nq = seq // tq
    gw = GQA_GROUP * HEAD_DIM
    rows = seq // GRID_W
    assert rows >= NA_WIN_ROWS and tq == 2 * GRID_W and rpb.shape[1:] == (NA_DR, NA_DC)
    table = _na_table(rpb)
    return pl.pallas_call(
        functools.partial(_attn_nbr_body, rows=rows),
        grid=(n_kv, batch, nq),
        in_specs=[pl.BlockSpec((tq, gw), lambda h, b, i: (b * nq + i, h)),
                  pl.BlockSpec((seq, HEAD_DIM), lambda h, b, i: (b, k_off + h)),
                  pl.BlockSpec((seq, HEAD_DIM), lambda h, b, i: (b, v_off + h)),
                  pl.BlockSpec((GQA_GROUP, NA_PAIRS, GRID_W, 2 * GRID_W), lambda h, b, i: (h, 0, 0, 0))],
        out_specs=pl.BlockSpec((tq, gw), lambda h, b, i: (b * nq + i, h)),
        out_shape=jax.ShapeDtypeStruct((n, D_MODEL), BF16),
        compiler_params=_params("parallel", "parallel", "parallel"),
        name="attn_nbr",
    )(qkv, qkv, qkv, table)


def _sgu_body(u_ref, v_ref, vg_ref, ws_ref, bs_ref, o_ref):
    v = v_ref[...]
    ms = jnp.mean(v * v, axis=-1, keepdims=True)
    vn = (v * lax.rsqrt(ms + RMS_EPS) * vg_ref[...]).astype(BF16)
    cg = v.shape[1] // SG_GROUPS
    for g in range(SG_GROUPS):
        mixed = jnp.dot(ws_ref[g], vn[:, g * cg:(g + 1) * cg], preferred_element_type=F32)
        mixed = mixed + bs_ref[:, g:g + 1]
        o_ref[:, g * cg:(g + 1) * cg] = (u_ref[:, g * cg:(g + 1) * cg] * mixed).astype(o_ref.dtype)


def _sgu(z, v_gain, w_s, b_s):
    n = z.shape[0]
    width = z.shape[1] // 2
    return pl.pallas_call(
        _sgu_body,
        grid=(n // SG_CHUNK,),
        in_specs=[pl.BlockSpec((SG_CHUNK, width), lambda c: (c, 0)),
                  pl.BlockSpec((SG_CHUNK, width), lambda c: (c, 1)),
                  pl.BlockSpec((1, width), lambda c: (0, 0)),
                  pl.BlockSpec((SG_GROUPS, SG_CHUNK, SG_CHUNK), lambda c: (0, 0, 0)),
                  pl.BlockSpec((SG_CHUNK, SG_GROUPS), lambda c: (0, 0))],
        out_specs=pl.BlockSpec((SG_CHUNK, width), lambda c: (c, 0)),
        out_shape=jax.ShapeDtypeStruct((n, width), BF16),
        compiler_params=_params("parallel"),
        name="sgu",
    )(z, z, v_gain.reshape(1, width), w_s.astype(BF16), b_s.T)


def _pack_bf16_pair(lo, hi):
    lo_bits = lax.bitcast_convert_type(lo.astype(BF16).astype(F32), PACKED)
    hi_bits = lax.bitcast_convert_type(hi.astype(BF16).astype(F32), PACKED)
    return (hi_bits & jnp.uint32(0xFFFF0000)) | (lo_bits >> 16)


def _unpack_bf16_pair(words):
    lo = lax.bitcast_convert_type(words << 16, F32)
    hi = lax.bitcast_convert_type(words & jnp.uint32(0xFFFF0000), F32)
    return lo.astype(BF16), hi.astype(BF16)


def _router_body(x_ref, g_ref, wr_ref, br_ref, xn_ref, ids_ref, gates_ref, counts_ref, carry):
    @pl.when(pl.program_id(0) == 0)
    def _():
        carry[...] = jnp.zeros(carry.shape, carry.dtype)

    x = x_ref[...]
    ms = jnp.mean(x * x, axis=-1, keepdims=True)
    xn = x * lax.rsqrt(ms + RMS_EPS) * g_ref[...]
    half_d = xn.shape[1] // 2
    xn_ref[...] = _pack_bf16_pair(xn[:, :half_d], xn[:, half_d:])
    logits = jnp.dot(xn, wr_ref[...], preferred_element_type=F32,
                     precision=lax.Precision.HIGHEST) + br_ref[...]
    lane = lax.broadcasted_iota(jnp.int32, logits.shape, 1)
    ninf = jnp.float32(-jnp.inf)

    lg = jnp.where(lane < N_GROUPS, logits, ninf)
    mg = jnp.max(lg, axis=-1, keepdims=True)
    grp = jnp.min(jnp.where(lg == mg, lane, LANES), axis=-1, keepdims=True)
    p_grp = 1.0 / jnp.sum(jnp.exp(lg - mg), axis=-1, keepdims=True)

    lo = N_GROUPS + grp * EXPERTS_PER_GROUP
    le = jnp.where((lane >= lo) & (lane < lo + EXPERTS_PER_GROUP), logits, ninf)
    m1 = jnp.max(le, axis=-1, keepdims=True)
    i1 = jnp.min(jnp.where(le == m1, lane, LANES), axis=-1, keepdims=True)
    le2 = jnp.where(lane == i1, ninf, le)
    m2 = jnp.max(le2, axis=-1, keepdims=True)
    i2 = jnp.min(jnp.where(le2 == m2, lane, LANES), axis=-1, keepdims=True)
    z = jnp.sum(jnp.exp(le - m1), axis=-1, keepdims=True)
    p1 = 1.0 / z
    p2 = jnp.exp(m2 - m1) / z
    tot = p1 + p2
    g1 = p_grp * (p1 / tot)
    g2 = p_grp * (p2 / tot)
    gates_ref[...] = jnp.where(lane == 0, g1, jnp.where(lane == 1, g2, 0.0))

    e0 = i1 - N_GROUPS
    e1 = i2 - N_GROUPS
    tm = x.shape[0]
    onehot = jnp.where((lane == e0) | (lane == e1), 1.0, 0.0)
    tri = jnp.where(lax.broadcasted_iota(jnp.int32, (tm, tm), 0) > lax.broadcasted_iota(jnp.int32, (tm, tm), 1),
                    1.0, 0.0).astype(BF16)
    before = jnp.dot(tri, onehot.astype(BF16), preferred_element_type=F32) + carry[...]
    r0 = jnp.sum(jnp.where(lane == e0, before, 0.0), axis=-1, keepdims=True).astype(jnp.int32)
    r1 = jnp.sum(jnp.where(lane == e1, before, 0.0), axis=-1, keepdims=True).astype(jnp.int32)
    carry[...] = carry[...] + jnp.sum(onehot, axis=0, keepdims=True)
    counts_ref[...] = carry[...].astype(jnp.int32)
    ids_ref[...] = jnp.where(lane == 0, e0, jnp.where(lane == 1, e1, jnp.where(lane == 2, r0, jnp.where(lane == 3, r1, 0))))


def _router(h, gain, w_group, b_group, w_expert, b_expert):
    n, d = h.shape
    tm = min(256, n)
    wr = jnp.concatenate([w_group, w_expert.transpose(1, 0, 2).reshape(d, N_EXPERTS)], axis=1)
    wr = jnp.pad(wr.astype(F32), ((0, 0), (0, LANES - wr.shape[1])))
    br = jnp.concatenate([b_group, b_expert.reshape(N_EXPERTS)]).astype(F32)
    br = jnp.pad(br, (0, LANES - br.shape[0])).reshape(1, LANES)
    return pl.pallas_call(
        _router_body,
        grid=(n // tm,),
        in_specs=[pl.BlockSpec((tm, d), lambda i: (i, 0)),
                  pl.BlockSpec((1, d), lambda i: (0, 0)),
                  pl.BlockSpec((d, LANES), lambda i: (0, 0)),
                  pl.BlockSpec((1, LANES), lambda i: (0, 0))],
        out_specs=[pl.BlockSpec((tm, d // 2), lambda i: (i, 0)),
                   pl.BlockSpec((tm, LANES), lambda i: (i, 0)),
                   pl.BlockSpec((tm, LANES), lambda i: (i, 0)),
                   pl.BlockSpec((1, LANES), lambda i: (0, 0))],
        out_shape=[jax.ShapeDtypeStruct((n, d // 2), PACKED),
                   jax.ShapeDtypeStruct((n, LANES), jnp.int32),
                   jax.ShapeDtypeStruct((n, LANES), F32),
                   jax.ShapeDtypeStruct((1, LANES), jnp.int32)],
        scratch_shapes=[pltpu.VMEM((1, LANES), F32)],
        compiler_params=_params("arbitrary"),
        name="norm_router",
    )(h, gain.reshape(1, d), wr, br)


def _dispatch_plan(ids, counts):
    n = ids.shape[0]
    nka = n * 2
    nb = (nka + MOE_ROWS - 1) // MOE_ROWS + N_EXPERTS + 1
    e, rank = ids[:, :2], ids[:, 2:4]
    blocks = (counts + MOE_ROWS - 1) // MOE_ROWS
    blk_end = jnp.cumsum(blocks)
    blk_start = blk_end - blocks
    n_used = blk_end[-1]
    lanes = jnp.arange(N_EXPERTS, dtype=jnp.int32)
    row_start = jnp.sum(jnp.where(e[:, :, None] == lanes, blk_start * MOE_ROWS, 0), axis=-1)
    dest = (row_start + rank).astype(jnp.int32).reshape(nka)
    flat_tok = jnp.arange(nka, dtype=jnp.int32) // 2
    row_tok = jnp.zeros((nb * MOE_ROWS,), jnp.int32).at[dest].set(flat_tok)
    b = jnp.arange(nb, dtype=jnp.int32)
    block_e = jnp.minimum(jnp.sum(b[:, None] >= blk_end[None, :], axis=1), N_EXPERTS - 1).astype(jnp.int32)
    first = ((b == blk_start[block_e]) & (b < n_used)).astype(jnp.int32)
    nxt_blk = blk_end[block_e]
    nxt_e = jnp.where(nxt_blk < n_used, block_e[jnp.minimum(nxt_blk, nb - 1)], -1).astype(jnp.int32)
    return dest, row_tok, block_e, first, nxt_e, n_used.astype(jnp.int32).reshape(1), nb


def _row_copy(src_hbm, row, dst, dst_row, sem):
    return pltpu.make_async_copy(src_hbm.at[pl.ds(row, 1), :], dst.at[pl.ds(dst_row, 1), :], sem)


def _experts_body(be_ref, first_ref, nxt_ref, tok_ref, nu_ref, x_hbm, w1_hbm, w3_hbm, w2_hbm, o_ref,
                  xbuf, st13, st2, wb13, wb2, gsem, wsem, *, layer):
    i = pl.program_id(0)
    n_used = nu_ref[0]
    rows = xbuf.shape[1]
    de = st2.shape[0]
    slot = i % 2

    def weight_copies(e):
        return (pltpu.make_async_copy(w1_hbm.at[layer, e], st13.at[0], wsem.at[0]),
                pltpu.make_async_copy(w3_hbm.at[layer, e], st13.at[1], wsem.at[1]),
                pltpu.make_async_copy(w2_hbm.at[layer, e], st2, wsem.at[2]))

    def start_gather(blk, dst_slot, lo=0, hi=rows):
        for r in range(lo, hi):
            _row_copy(x_hbm, tok_ref[blk * rows + r], xbuf.at[dst_slot], r, gsem.at[dst_slot]).start()

    @pl.when(i == 0)
    def _():
        for c in weight_copies(be_ref[0]):
            c.start(priority=WEIGHT_DMA_PRIORITY)
        start_gather(0, 0)

    @pl.when(i <= n_used)
    def _():
        for r in range(rows):
            _row_copy(x_hbm, 0, xbuf.at[slot], r, gsem.at[slot]).wait()

    @pl.when(i < n_used)
    def _():
        @pl.when(first_ref[i] == 1)
        def _():
            for c in weight_copies(0):
                c.wait()
            wb13[:, :de] = st13[0].astype(BF16)
            wb13[:, de:] = st13[1].astype(BF16)
            wb2[...] = st2[...].astype(BF16)

            @pl.when(nxt_ref[i] >= 0)
            def _():
                for c in weight_copies(nxt_ref[i]):
                    c.start(priority=WEIGHT_DMA_PRIORITY)

        pieces = 8
        per = rows // pieces
        x = jnp.concatenate(_unpack_bf16_pair(xbuf[slot]), axis=1)
        half = rows // 2
        wide = (2 * de) // (2 * MXU_TILE) * (2 * MXU_TILE)
        start_gather(i + 1, 1 - slot, 0, per)
        h_a = jnp.dot(x, wb13[:, :wide], preferred_element_type=F32)
        start_gather(i + 1, 1 - slot, per, 2 * per)
        h_b = jnp.concatenate(
            [jnp.dot(x[:half], wb13[:, wide:], preferred_element_type=F32),
             jnp.dot(x[half:], wb13[:, wide:], preferred_element_type=F32)], axis=0)
        h13 = jnp.concatenate([h_a, h_b], axis=1)
        start_gather(i + 1, 1 - slot, 2 * per, 4 * per)
        h1, h3 = h13[:, :de], h13[:, de:]
        a = ((h1 / (1.0 + jnp.exp(-h1))) * h3).astype(BF16)
        d = o_ref.shape[1]
        cols = d // 4
        for c in range(4):
            o_ref[:, c * cols:(c + 1) * cols] = jnp.dot(a, wb2[:, c * cols:(c + 1) * cols],
                                                        preferred_element_type=F32)
            start_gather(i + 1, 1 - slot, (4 + c) * per, (5 + c) * per)

    @pl.when(i >= n_used)
    def _():
        o_ref[...] = jnp.zeros(o_ref.shape, o_ref.dtype)


def _experts(xn, row_tok, block_e, first, nxt_e, n_used, nb, w1, w3, w2, *, layer):
    d, de = w1.shape[2:]
    assert xn.shape[1] * 2 == d
    grid_spec = pltpu.PrefetchScalarGridSpec(
        num_scalar_prefetch=5,
        grid=(nb,),
        in_specs=[pl.BlockSpec(memory_space=pl.ANY)] * 4,
        out_specs=pl.BlockSpec((MOE_ROWS, d), lambda i, *_: (i, 0)),
        scratch_shapes=[pltpu.VMEM((2, MOE_ROWS, d // 2), PACKED),
                        pltpu.VMEM((2, d, de), F32),
                        pltpu.VMEM((de, d), F32),
                        pltpu.VMEM((d, 2 * de), BF16),
                        pltpu.VMEM((de, d), BF16),
                        pltpu.SemaphoreType.DMA((2,)),
                        pltpu.SemaphoreType.DMA((3,))],
    )
    return pl.pallas_call(
        functools.partial(_experts_body, layer=layer),
        grid_spec=grid_spec,
        out_shape=jax.ShapeDtypeStruct((nb * MOE_ROWS, d), F32),
        compiler_params=pltpu.CompilerParams(dimension_semantics=("arbitrary",),
                                             vmem_limit_bytes=EXPERTS_VMEM_LIMIT),
        name="experts",
    )(block_e, first, nxt_e, row_tok, n_used, xn, w1, w3, w2)


def _combine_body(pos_ref, h_ref, g_ref, gain_ref, y_hbm, *rest, last):
    if last:
        n_ref, ybuf, sem = rest
    else:
        o_ref, n_ref, ybuf, sem = rest
    i = pl.program_id(0)
    n_steps = pl.num_programs(0)
    tm = h_ref.shape[0]

    def start_gather(step, slot):
        for r in range(tm):
            base = (step * tm + r) * 2
            _row_copy(y_hbm, pos_ref[base], ybuf.at[slot, 0], r, sem.at[slot]).start()
            _row_copy(y_hbm, pos_ref[base + 1], ybuf.at[slot, 1], r, sem.at[slot]).start()

    def wait_gather(slot):
        for r in range(tm):
            _row_copy(y_hbm, 0, ybuf.at[slot, 0], r, sem.at[slot]).wait()
            _row_copy(y_hbm, 0, ybuf.at[slot, 1], r, sem.at[slot]).wait()

    slot = i % 2

    @pl.when(i == 0)
    def _():
        start_gather(0, 0)

    @pl.when(i + 1 < n_steps)
    def _():
        start_gather(i + 1, 1 - slot)

    wait_gather(slot)
    g = g_ref[...]
    out = h_ref[...] + (g[:, 0:1] * ybuf[slot, 0] + g[:, 1:2] * ybuf[slot, 1])
    if not last:
        o_ref[...] = out
    ms = jnp.mean(out * out, axis=-1, keepdims=True)
    n_ref[...] = (out * lax.rsqrt(ms + RMS_EPS) * gain_ref[...]).astype(n_ref.dtype)


def _combine(h, gates, pos, yb, next_gain, *, last):
    n, d = h.shape
    tm = min(128, n)
    row_spec = pl.BlockSpec((tm, d), lambda i, pos_: (i, 0))
    grid_spec = pltpu.PrefetchScalarGridSpec(
        num_scalar_prefetch=1,
        grid=(n // tm,),
        in_specs=[row_spec,
                  pl.BlockSpec((tm, LANES), lambda i, pos_: (i, 0)),
                  pl.BlockSpec((1, d), lambda i, pos_: (0, 0)),
                  pl.BlockSpec(memory_space=pl.ANY)],
        out_specs=[row_spec] if last else [row_spec, row_spec],
        scratch_shapes=[pltpu.VMEM((2, 2, tm, d), F32),
                        pltpu.SemaphoreType.DMA((2,))],
    )
    normed = jax.ShapeDtypeStruct((n, d), F32 if last else BF16)
    return pl.pallas_call(
        functools.partial(_combine_body, last=last),
        grid_spec=grid_spec,
        out_shape=[normed] if last else [jax.ShapeDtypeStruct((n, d), F32), normed],
        compiler_params=_params("arbitrary"),
        name="moe_combine",
    )(pos, h, gates, next_gain.reshape(1, d), yb)


def _hierarchical_moe(h, gain, w_group, b_group, w_expert, b_expert, w1, w3, w2, next_gain, *, layer, last):
    xn, ids, gates, counts = _router(h, gain, w_group, b_group, w_expert, b_expert)
    dest, row_tok, block_e, first, nxt_e, n_used, nb = _dispatch_plan(ids[:, :4], counts[0, :N_EXPERTS])
    yb = _experts(xn, row_tok, block_e, first, nxt_e, n_used, nb, w1, w3, w2, layer=layer)
    return _combine(h, gates, dest, yb, next_gain, last=last)


def kernel(x, norm_mix, norm_ffn, norm_final, a_w_in, a_q_gain, a_k_gain, a_w_o, b_w_in, b_v_gain, b_w_s, b_bias, b_w_o, c_w_in, c_sink, c_w_o, d_w_in, d_rpb, d_w_o, moe_w_group, moe_b_group, moe_w_expert, moe_b_expert, moe_w1, moe_w3, moe_w2):
    batch, seq, d = x.shape
    h = x.reshape(batch * seq, d)
    ones = jnp.ones((HEAD_DIM,), F32)
    depth = norm_mix.shape[0]
    hn = _rmsnorm(h, norm_mix[0], BF16)
    for i in range(depth):
        m, j = i % N_MIXERS, i // N_MIXERS
        if m == 0:
            cos, sin = _axial_tables(seq)
            qkv = _matmul_qkv(hn, a_w_in[j], cos, sin, a_q_gain[j], a_k_gain[j],
                              mode="axial", seq=seq)
            mixed, w_o = _attn_dense(qkv, seq), a_w_o[j]
        elif m == 1:
            z = _matmul_gelu(hn, b_w_in[j], F32)
            mixed, w_o = _sgu(z, b_v_gain[j], b_w_s[j], b_bias[j]), b_w_o[j]
        elif m == 2:
            cos, sin = _rope_tables(seq)
            qkv = _matmul_qkv(hn, c_w_in[j], cos, sin, ones, ones, mode="rope", seq=seq)
            mixed, w_o = _attn_window(qkv, c_sink[j].astype(F32), seq), c_w_o[j]
        else:
            cos, sin = _rope_tables(seq)
            qkv = _matmul_qkv(hn, d_w_in[j], cos, sin, ones, ones, mode="none", seq=seq)
            mixed, w_o = _attn_nbr(qkv, d_rpb[j], seq), d_w_o[j]
        h = _matmul_residual(mixed, w_o, h)
        last = i == depth - 1
        outs = _hierarchical_moe(h, norm_ffn[i], moe_w_group[i], moe_b_group[i], moe_w_expert[i],
                                 moe_b_expert[i], moe_w1, moe_w3, moe_w2,
                                 norm_final if last else norm_mix[i + 1], layer=i, last=last)
        if last:
            return outs[0].reshape(batch, seq, d)
        h, hn = outs
```

```python
import functools

import jax
import jax.numpy as jnp
from jax import lax
from jax.experimental import pallas as pl
from jax.experimental.pallas import tpu as pltpu

D_MODEL = 4096
BATCH = 2
SEQ = 4096
DEPTH = 4
N_MIXERS = 4
HEAD_DIM = 128
GQA_GROUP = 4
ROPE_THETA = 10000.0
Q_BLOCK = 128
GRID_W = 64
SG_CHUNK = 128
SG_GROUPS = 8
WINDOW = 128
NA_ROWS = 8
NA_COLS = 16
N_GROUPS = 8
EXPERTS_PER_GROUP = 4
N_EXPERTS = N_GROUPS * EXPERTS_PER_GROUP
D_EXPERT = 384
MOE_ROWS = 256
GATHER_AHEAD = 2
RMS_EPS = 1e-6
NEG_INF = -1e30

LOG2_E = 1.4426950408889634
LANES = 128
MXU_TILE = 256
WEIGHT_DMA_PRIORITY = 1
DENSE_KEY_CHUNK = 1024
DENSE_Q_ROWS = 256
NA_WIN_ROWS = 10
VMEM_LIMIT = 48 * 1024 * 1024
EXPERTS_VMEM_LIMIT = 58 * 1024 * 1024

F32 = jnp.float32
BF16 = jnp.bfloat16
PACKED = jnp.uint32


def _params(*sem):
    return pltpu.CompilerParams(dimension_semantics=sem, vmem_limit_bytes=VMEM_LIMIT)


def _rmsnorm_body(x_ref, g_ref, o_ref):
    x = x_ref[...]
    ms = jnp.mean(x * x, axis=-1, keepdims=True)
    o_ref[...] = (x * lax.rsqrt(ms + RMS_EPS) * g_ref[...]).astype(o_ref.dtype)


def _rmsnorm(x, gain, out_dtype):
    n, d = x.shape
    tm = min(256, n)
    return pl.pallas_call(
        _rmsnorm_body,
        grid=(n // tm,),
        in_specs=[pl.BlockSpec((tm, d), lambda i: (i, 0)),
                  pl.BlockSpec((1, d), lambda i: (0, 0))],
        out_specs=pl.BlockSpec((tm, d), lambda i: (i, 0)),
        out_shape=jax.ShapeDtypeStruct((n, d), out_dtype),
        compiler_params=_params("parallel"),
        name="rmsnorm",
    )(x, gain.reshape(1, d))


def _gelu_tanh(x):
    return 0.5 * x * (1.0 + jnp.tanh(0.7978845608028654 * (x + 0.044715 * (x * x * x))))


def _mm_gelu_body(x_ref, w_ref, o_ref):
    acc = jnp.dot(x_ref[...], w_ref[...].astype(BF16), preferred_element_type=F32)
    o_ref[...] = _gelu_tanh(acc).astype(o_ref.dtype)


def _mm_res_body(x_ref, w_ref, r_ref, o_ref):
    acc = jnp.dot(x_ref[...], w_ref[...].astype(BF16), preferred_element_type=F32)
    o_ref[...] = r_ref[...] + acc


def _swap_halves(y, width):
    if 2 * width == LANES:
        return pltpu.roll(y, width, axis=1)
    lane = lax.broadcasted_iota(jnp.int32, y.shape, 1)
    first = (lane % (2 * width)) < width
    return jnp.where(first, pltpu.roll(y, LANES - width, axis=1), pltpu.roll(y, width, axis=1))


def _mm_qkv_body(x_ref, w_ref, cos_ref, sin_ref, gq_ref, gk_ref, o_ref, *, mode, n_q, n_qk, scale):
    acc = jnp.dot(x_ref[...], w_ref[...].astype(BF16), preferred_element_type=F32)
    j = pl.program_id(1)
    heads = acc.shape[1] // HEAD_DIM
    is_q = j < n_q

    @pl.when(j < n_qk)
    def _():
        if mode == "axial":
            gain = jnp.where(is_q, gq_ref[...] * scale, gk_ref[...])
        else:
            gain = jnp.where(is_q, jnp.full((1, HEAD_DIM), scale, F32), jnp.ones((1, HEAD_DIM), F32))
        for t in range(heads):
            a = acc[:, t * HEAD_DIM:(t + 1) * HEAD_DIM]
            if mode == "axial":
                a = a * lax.rsqrt(jnp.mean(a * a, axis=-1, keepdims=True) + RMS_EPS)
            a = a * gain
            if mode == "axial":
                a = a * cos_ref[...] + _swap_halves(a, HEAD_DIM // 4) * sin_ref[...]
            elif mode == "rope":
                a = a * cos_ref[...] + _swap_halves(a, HEAD_DIM // 2) * sin_ref[...]
            o_ref[:, t * HEAD_DIM:(t + 1) * HEAD_DIM] = a.astype(o_ref.dtype)

    @pl.when(j >= n_qk)
    def _():
        o_ref[...] = acc.astype(o_ref.dtype)


def _mm_tiles(m, n):
    return min(1024, m), min(512, n)


def _matmul_gelu(x, w, out_dtype):
    m, k = x.shape
    n = w.shape[1]
    tm, tn = _mm_tiles(m, n)
    return pl.pallas_call(
        _mm_gelu_body,
        grid=(m // tm, n // tn),
        in_specs=[pl.BlockSpec((tm, k), lambda i, j: (i, 0)),
                  pl.BlockSpec((k, tn), lambda i, j: (0, j))],
        out_specs=pl.BlockSpec((tm, tn), lambda i, j: (i, j)),
        out_shape=jax.ShapeDtypeStruct((m, n), out_dtype),
        compiler_params=_params("parallel", "parallel"),
        name="matmul_gelu",
    )(x, w)


def _matmul_residual(x, w, res):
    m, k = x.shape
    n = w.shape[1]
    tm, tn = _mm_tiles(m, n)
    return pl.pallas_call(
        _mm_res_body,
        grid=(m // tm, n // tn),
        in_specs=[pl.BlockSpec((tm, k), lambda i, j: (i, 0)),
                  pl.BlockSpec((k, tn), lambda i, j: (0, j)),
                  pl.BlockSpec((tm, tn), lambda i, j: (i, j))],
        out_specs=pl.BlockSpec((tm, tn), lambda i, j: (i, j)),
        out_shape=jax.ShapeDtypeStruct((m, n), F32),
        compiler_params=_params("parallel", "parallel"),
        name="matmul_residual",
    )(x, w, res)


def _matmul_qkv(x, w, cos, sin, gq, gk, *, mode, seq):
    m, k = x.shape
    n = w.shape[1]
    d_q = D_MODEL
    d_kv = (n - d_q) // 2
    tm = min(1024, seq)
    tn = min(512, d_kv)
    s_blocks = seq // tm
    scale = HEAD_DIM ** -0.5 * (LOG2_E if mode == "axial" else 1.0)
    body = functools.partial(_mm_qkv_body, mode=mode, n_q=d_q // tn, n_qk=(d_q + d_kv) // tn, scale=scale)
    return pl.pallas_call(
        body,
        grid=(m // tm, n // tn),
        in_specs=[pl.BlockSpec((tm, k), lambda i, j: (i, 0)),
                  pl.BlockSpec((k, tn), lambda i, j: (0, j)),
                  pl.BlockSpec((tm, HEAD_DIM), lambda i, j: (i % s_blocks, 0)),
                  pl.BlockSpec((tm, HEAD_DIM), lambda i, j: (i % s_blocks, 0)),
                  pl.BlockSpec((1, HEAD_DIM), lambda i, j: (0, 0)),
                  pl.BlockSpec((1, HEAD_DIM), lambda i, j: (0, 0))],
        out_specs=pl.BlockSpec((tm, tn), lambda i, j: (i, j)),
        out_shape=jax.ShapeDtypeStruct((m, n), BF16),
        compiler_params=_params("parallel", "parallel"),
        name="matmul_qkv_" + mode,
    )(x, w, cos, sin, gq.reshape(1, HEAD_DIM), gk.reshape(1, HEAD_DIM))


def _rope_cos_sin(pos, dim):
    half = dim // 2
    inv = jnp.power(jnp.float32(ROPE_THETA), -jnp.arange(half, dtype=F32) * (2.0 / dim))
    ang = pos.astype(F32)[:, None] * inv[None, :]
    return jnp.cos(ang), jnp.sin(ang)


def _axial_tables(seq):
    t = jnp.arange(seq, dtype=jnp.int32)
    cr, sr = _rope_cos_sin(t // GRID_W, HEAD_DIM // 2)
    cc, sc = _rope_cos_sin(t % GRID_W, HEAD_DIM // 2)
    return (jnp.concatenate([cr, cr, cc, cc], axis=-1),
            jnp.concatenate([-sr, sr, -sc, sc], axis=-1))


def _rope_tables(seq):
    c, s = _rope_cos_sin(jnp.arange(seq, dtype=jnp.int32), HEAD_DIM)
    return jnp.concatenate([c, c], axis=-1), jnp.concatenate([-s, s], axis=-1)


def _stack_heads(q_ref):
    return jnp.concatenate([q_ref[:, g * HEAD_DIM:(g + 1) * HEAD_DIM] for g in range(GQA_GROUP)], axis=0)


def _unstack_heads(o, o_ref):
    tq = o_ref.shape[0]
    for g in range(GQA_GROUP):
        o_ref[:, g * HEAD_DIM:(g + 1) * HEAD_DIM] = o[g * tq:(g + 1) * tq].astype(o_ref.dtype)


def _qk(q, k):
    return lax.dot_general(q, k, (((1,), (1,)), ((), ())), preferred_element_type=F32)


def _attn_dense_body(q_ref, k_ref, v_ref, o_ref):
    q = _stack_heads(q_ref)
    seq = k_ref.shape[0]
    kc = min(DENSE_KEY_CHUNK, seq)
    m = l = acc = None
    for c in range(seq // kc):
        s = _qk(q, k_ref[c * kc:(c + 1) * kc, :])
        mc = jnp.max(s, axis=-1, keepdims=True)
        if c == 0:
            m = mc
            p = jnp.exp2(s - m)
            l = jnp.sum(p, axis=-1, keepdims=True)
            acc = jnp.dot(p.astype(BF16), v_ref[c * kc:(c + 1) * kc, :], preferred_element_type=F32)
        else:
            m_new = jnp.maximum(m, mc)
            alpha = jnp.exp2(m - m_new)
            p = jnp.exp2(s - m_new)
            l = alpha * l + jnp.sum(p, axis=-1, keepdims=True)
            acc = alpha * acc + jnp.dot(p.astype(BF16), v_ref[c * kc:(c + 1) * kc, :],
                                        preferred_element_type=F32)
            m = m_new
    _unstack_heads(acc / l, o_ref)


def _attn_dense(qkv, seq):
    n = qkv.shape[0]
    batch = n // seq
    n_kv = D_MODEL // (GQA_GROUP * HEAD_DIM)
    k_off = D_MODEL // HEAD_DIM
    v_off = k_off + n_kv
    tq = DENSE_Q_ROWS
    nq = seq // tq
    gw = GQA_GROUP * HEAD_DIM
    return pl.pallas_call(
        _attn_dense_body,
        grid=(batch, n_kv, nq),
        in_specs=[pl.BlockSpec((tq, gw), lambda b, h, i: (b * nq + i, h)),
                  pl.BlockSpec((seq, HEAD_DIM), lambda b, h, i: (b, k_off + h)),
                  pl.BlockSpec((seq, HEAD_DIM), lambda b, h, i: (b, v_off + h))],
        out_specs=pl.BlockSpec((tq, gw), lambda b, h, i: (b * nq + i, h)),
        out_shape=jax.ShapeDtypeStruct((n, D_MODEL), BF16),
        compiler_params=_params("parallel", "parallel", "parallel"),
        name="attn_dense",
    )(qkv, qkv, qkv)


def _attn_window_body(sink_ref, q_ref, k_ref, v_ref, o_ref):
    h = pl.program_id(1)
    i = pl.program_id(2)
    tq = q_ref.shape[0]
    nkeys = tq + 2 * WINDOW
    start = pl.multiple_of(jnp.clip(i * tq - WINDOW, 0, k_ref.shape[0] - nkeys), WINDOW)
    k3 = k_ref[pl.ds(start, nkeys), :]
    v3 = v_ref[pl.ds(start, nkeys), :]
    s = _qk(_stack_heads(q_ref), k3)
    row = lax.broadcasted_iota(jnp.int32, s.shape, 0)
    col = lax.broadcasted_iota(jnp.int32, s.shape, 1)
    qpos = i * tq + row % tq
    kpos = start + col
    s = jnp.where(jnp.abs(qpos - kpos) <= WINDOW, s, NEG_INF)
    rowg = lax.broadcasted_iota(jnp.int32, (s.shape[0], 1), 0) // tq
    sink = jnp.zeros((s.shape[0], 1), F32)
    for g in range(GQA_GROUP):
        sink = jnp.where(rowg == g, sink_ref[h * GQA_GROUP + g], sink)
    m = jnp.maximum(jnp.max(s, axis=-1, keepdims=True), sink)
    p = jnp.exp(s - m)
    denom = jnp.sum(p, axis=-1, keepdims=True) + jnp.exp(sink - m)
    o = jnp.dot(p.astype(BF16), v3, preferred_element_type=F32) / denom
    _unstack_heads(o, o_ref)


def _attn_window(qkv, sink, seq):
    n = qkv.shape[0]
    batch = n // seq
    n_kv = D_MODEL // (GQA_GROUP * HEAD_DIM)
    k_off = D_MODEL // HEAD_DIM
    v_off = k_off + n_kv
    tq = 2 * Q_BLOCK
    nq = seq // tq
    gw = GQA_GROUP * HEAD_DIM
    assert seq >= tq + 2 * WINDOW
    grid_spec = pltpu.PrefetchScalarGridSpec(
        num_scalar_prefetch=1,
        grid=(batch, n_kv, nq),
        in_specs=[pl.BlockSpec((tq, gw), lambda b, h, i, s_: (b * nq + i, h)),
                  pl.BlockSpec((seq, HEAD_DIM), lambda b, h, i, s_: (b, k_off + h)),
                  pl.BlockSpec((seq, HEAD_DIM), lambda b, h, i, s_: (b, v_off + h))],
        out_specs=pl.BlockSpec((tq, gw), lambda b, h, i, s_: (b * nq + i, h)),
    )
    return pl.pallas_call(
        _attn_window_body,
        grid_spec=grid_spec,
        out_shape=jax.ShapeDtypeStruct((n, D_MODEL), BF16),
        compiler_params=_params("parallel", "parallel", "parallel"),
        name="attn_window",
    )(sink, qkv, qkv, qkv)


def _na_window_start(i, rows):
    return jnp.clip(2 * i - NA_ROWS // 2, 0, rows - NA_WIN_ROWS)


NA_DR = 2 * NA_ROWS - 1
NA_DC = 2 * NA_COLS - 1
NA_PAIR_LO = -2
NA_PAIRS = NA_DR + 3


def _na_table_body(rpb_ref, o_ref):
    h = pl.program_id(0)
    shape = (GRID_W, 2 * GRID_W)
    c = lax.broadcasted_iota(jnp.int32, shape, 0)
    lane = lax.broadcasted_iota(jnp.int32, shape, 1)
    kc = lane % GRID_W
    d = kc - c + (NA_COLS - 1)
    cs = jnp.clip(c - NA_COLS // 2, 0, GRID_W - NA_COLS)
    col_ok = (kc >= cs) & (kc < cs + NA_COLS)
    neg = jnp.full(shape, NEG_INF, F32)
    rows = []
    for a in range(NA_DR):
        val = jnp.zeros(shape, F32)
        for b in range(NA_DC):
            val = jnp.where(d == b, rpb_ref[(h * NA_DR + a) * NA_DC + b], val)
        rows.append(jnp.where(col_ok, val, neg))
    for e in range(NA_PAIRS):
        a0 = e + NA_PAIR_LO
        lo = rows[a0] if 0 <= a0 < NA_DR else neg
        hi = rows[a0 + 1] if 0 <= a0 + 1 < NA_DR else neg
        o_ref[0, e] = jnp.where(lane >= GRID_W, hi, lo)


def _na_table(rpb):
    n_heads = rpb.shape[0]
    grid_spec = pltpu.PrefetchScalarGridSpec(
        num_scalar_prefetch=1,
        grid=(n_heads,),
        in_specs=[],
        out_specs=pl.BlockSpec((1, NA_PAIRS, GRID_W, 2 * GRID_W), lambda h, r: (h, 0, 0, 0)),
    )
    return pl.pallas_call(
        _na_table_body,
        grid_spec=grid_spec,
        out_shape=jax.ShapeDtypeStruct((n_heads, NA_PAIRS, GRID_W, 2 * GRID_W), F32),
        compiler_params=_params("arbitrary"),
        name="na_bias_table",
    )(rpb.astype(F32).reshape(-1))


def _attn_nbr_body(q_ref, k_ref, v_ref, t_ref, o_ref, *, rows):
    i = pl.program_id(2)
    ws = _na_window_start(i, rows)
    start = pl.multiple_of(ws * GRID_W, GRID_W)
    nkeys = NA_WIN_ROWS * GRID_W
    kw = k_ref[pl.ds(start, nkeys), :]
    vw = v_ref[pl.ds(start, nkeys), :]
    lane = lax.broadcasted_iota(jnp.int32, (1, 2 * GRID_W), 1)
    blocks = [[] for _ in range(GQA_GROUP)]
    for u in range(Q_BLOCK // GRID_W):
        r = (Q_BLOCK // GRID_W) * i + u
        rs = jnp.clip(r - NA_ROWS // 2, 0, rows - NA_ROWS)
        tiles = [[] for _ in range(GQA_GROUP)]
        for jp in range(NA_WIN_ROWS // 2):
            kr = ws + 2 * jp
            ok0 = (kr >= rs) & (kr < rs + NA_ROWS)
            ok1 = (kr + 1 >= rs) & (kr + 1 < rs + NA_ROWS)
            pen = jnp.where(lane < GRID_W, jnp.where(ok0, 0.0, NEG_INF), jnp.where(ok1, 0.0, NEG_INF))
            e = kr - r + (NA_ROWS - 1) - NA_PAIR_LO
            for g in range(GQA_GROUP):
                tiles[g].append(t_ref[g, e] + pen)
        for g in range(GQA_GROUP):
            blocks[g].append(jnp.concatenate(tiles[g], axis=1))
    bias = jnp.concatenate([b for g in range(GQA_GROUP) for b in blocks[g]], axis=0)
    s = _qk(_stack_heads(q_ref), kw) + bias
    m = jnp.max(s, axis=-1, keepdims=True)
    p = jnp.exp(s - m)
    l = jnp.sum(p, axis=-1, keepdims=True)
    o = jnp.dot(p.astype(BF16), vw, preferred_element_type=F32) / l
    _unstack_heads(o, o_ref)


def _attn_nbr(qkv, rpb, seq):
    n = qkv.shape[0]
    batch = n // seq
    n_kv = D_MODEL // (GQA_GROUP * HEAD_DIM)
    k_off = D_MODEL // HEAD_DIM
    v_off = k_off + n_kv
    tq = Q_BLOCK
    nq = seq // tq
    gw = GQA_GROUP * HEAD_DIM
    rows = seq // GRID_W
    assert rows >= NA_WIN_ROWS and tq == 2 * GRID_W and rpb.shape[1:] == (NA_DR, NA_DC)
    table = _na_table(rpb)
    return pl.pallas_call(
        functools.partial(_attn_nbr_body, rows=rows),
        grid=(n_kv, batch, nq),
        in_specs=[pl.BlockSpec((tq, gw), lambda h, b, i: (b * nq + i, h)),
                  pl.BlockSpec((seq, HEAD_DIM), lambda h, b, i: (b, k_off + h)),
                  pl.BlockSpec((seq, HEAD_DIM), lambda h, b, i: (b, v_off + h)),
                  pl.BlockSpec((GQA_GROUP, NA_PAIRS, GRID_W, 2 * GRID_W), lambda h, b, i: (h, 0, 0, 0))],
        out_specs=pl.BlockSpec((tq, gw), lambda h, b, i: (b * nq + i, h)),
        out_shape=jax.ShapeDtypeStruct((n, D_MODEL), BF16),
        compiler_params=_params("parallel", "parallel", "parallel"),
        name="attn_nbr",
    )(qkv, qkv, qkv, table)


def _sgu_body(u_ref, v_ref, vg_ref, ws_ref, bs_ref, o_ref):
    v = v_ref[...]
    ms = jnp.mean(v * v, axis=-1, keepdims=True)
    vn = (v * lax.rsqrt(ms + RMS_EPS) * vg_ref[...]).astype(BF16)
    cg = v.shape[1] // SG_GROUPS
    for g in range(SG_GROUPS):
        mixed = jnp.dot(ws_ref[g], vn[:, g * cg:(g + 1) * cg], preferred_element_type=F32)
        mixed = mixed + bs_ref[:, g:g + 1]
        o_ref[:, g * cg:(g + 1) * cg] = (u_ref[:, g * cg:(g + 1) * cg] * mixed).astype(o_ref.dtype)


def _sgu(z, v_gain, w_s, b_s):
    n = z.shape[0]
    width = z.shape[1] // 2
    return pl.pallas_call(
        _sgu_body,
        grid=(n // SG_CHUNK,),
        in_specs=[pl.BlockSpec((SG_CHUNK, width), lambda c: (c, 0)),
                  pl.BlockSpec((SG_CHUNK, width), lambda c: (c, 1)),
                  pl.BlockSpec((1, width), lambda c: (0, 0)),
                  pl.BlockSpec((SG_GROUPS, SG_CHUNK, SG_CHUNK), lambda c: (0, 0, 0)),
                  pl.BlockSpec((SG_CHUNK, SG_GROUPS), lambda c: (0, 0))],
        out_specs=pl.BlockSpec((SG_CHUNK, width), lambda c: (c, 0)),
        out_shape=jax.ShapeDtypeStruct((n, width), BF16),
        compiler_params=_params("parallel"),
        name="sgu",
    )(z, z, v_gain.reshape(1, width), w_s.astype(BF16), b_s.T)


def _pack_bf16_pair(lo, hi):
    lo_bits = lax.bitcast_convert_type(lo.astype(BF16).astype(F32), PACKED)
    hi_bits = lax.bitcast_convert_type(hi.astype(BF16).astype(F32), PACKED)
    return (hi_bits & jnp.uint32(0xFFFF0000)) | (lo_bits >> 16)


def _unpack_bf16_pair(words):
    lo = lax.bitcast_convert_type(words << 16, F32)
    hi = lax.bitcast_convert_type(words & jnp.uint32(0xFFFF0000), F32)
    return lo.astype(BF16), hi.astype(BF16)


def _router_body(x_ref, g_ref, wr_ref, br_ref, xn_ref, ids_ref, gates_ref, counts_ref, carry):
    @pl.when(pl.program_id(0) == 0)
    def _():
        carry[...] = jnp.zeros(carry.shape, carry.dtype)

    x = x_ref[...]
    ms = jnp.mean(x * x, axis=-1, keepdims=True)
    xn = x * lax.rsqrt(ms + RMS_EPS) * g_ref[...]
    half_d = xn.shape[1] // 2
    xn_ref[...] = _pack_bf16_pair(xn[:, :half_d], xn[:, half_d:])
    xh = xn.astype(BF16)
    xl = (xn - xh.astype(F32)).astype(BF16)
    w = wr_ref[...]
    wh = w.astype(BF16)
    wl = (w - wh.astype(F32)).astype(BF16)
    hi = jnp.dot(xh, jnp.concatenate([wh, wl], axis=1), preferred_element_type=F32)
    lo = jnp.dot(xl, wh, preferred_element_type=F32)
    logits = (hi[:, :LANES] + (hi[:, LANES:] + lo)) + br_ref[...]
    lane = lax.broadcasted_iota(jnp.int32, logits.shape, 1)
    ninf = jnp.float32(-jnp.inf)

    lg = jnp.where(lane < N_GROUPS, logits, ninf)
    mg = jnp.max(lg, axis=-1, keepdims=True)
    grp = jnp.min(jnp.where(lg == mg, lane, LANES), axis=-1, keepdims=True)
    p_grp = 1.0 / jnp.sum(jnp.exp(lg - mg), axis=-1, keepdims=True)

    lo = N_GROUPS + grp * EXPERTS_PER_GROUP
    le = jnp.where((lane >= lo) & (lane < lo + EXPERTS_PER_GROUP), logits, ninf)
    m1 = jnp.max(le, axis=-1, keepdims=True)
    i1 = jnp.min(jnp.where(le == m1, lane, LANES), axis=-1, keepdims=True)
    le2 = jnp.where(lane == i1, ninf, le)
    m2 = jnp.max(le2, axis=-1, keepdims=True)
    i2 = jnp.min(jnp.where(le2 == m2, lane, LANES), axis=-1, keepdims=True)
    z = jnp.sum(jnp.exp(le - m1), axis=-1, keepdims=True)
    p1 = 1.0 / z
    p2 = jnp.exp(m2 - m1) / z
    tot = p1 + p2
    g1 = p_grp * (p1 / tot)
    g2 = p_grp * (p2 / tot)
    gates_ref[...] = jnp.where(lane == 0, g1, jnp.where(lane == 1, g2, 0.0))

    e0 = i1 - N_GROUPS
    e1 = i2 - N_GROUPS
    tm = x.shape[0]
    onehot = jnp.where((lane == e0) | (lane == e1), 1.0, 0.0)
    tri = jnp.where(lax.broadcasted_iota(jnp.int32, (tm, tm), 0) > lax.broadcasted_iota(jnp.int32, (tm, tm), 1),
                    1.0, 0.0).astype(BF16)
    before = jnp.dot(tri, onehot.astype(BF16), preferred_element_type=F32) + carry[...]
    r0 = jnp.sum(jnp.where(lane == e0, before, 0.0), axis=-1, keepdims=True).astype(jnp.int32)
    r1 = jnp.sum(jnp.where(lane == e1, before, 0.0), axis=-1, keepdims=True).astype(jnp.int32)
    carry[...] = carry[...] + jnp.sum(onehot, axis=0, keepdims=True)
    counts_ref[...] = carry[...].astype(jnp.int32)
    ids_ref[...] = jnp.where(lane == 0, e0, jnp.where(lane == 1, e1, jnp.where(lane == 2, r0, jnp.where(lane == 3, r1, 0))))


def _router(h, gain, w_group, b_group, w_expert, b_expert):
    n, d = h.shape
    tm = min(256, n)
    wr = jnp.concatenate([w_group, w_expert.transpose(1, 0, 2).reshape(d, N_EXPERTS)], axis=1)
    wr = jnp.pad(wr.astype(F32), ((0, 0), (0, LANES - wr.shape[1])))
    br = jnp.concatenate([b_group, b_expert.reshape(N_EXPERTS)]).astype(F32)
    br = jnp.pad(br, (0, LANES - br.shape[0])).reshape(1, LANES)
    return pl.pallas_call(
        _router_body,
        grid=(n // tm,),
        in_specs=[pl.BlockSpec((tm, d), lambda i: (i, 0)),
                  pl.BlockSpec((1, d), lambda i: (0, 0)),
                  pl.BlockSpec((d, LANES), lambda i: (0, 0)),
                  pl.BlockSpec((1, LANES), lambda i: (0, 0))],
        out_specs=[pl.BlockSpec((tm, d // 2), lambda i: (i, 0)),
                   pl.BlockSpec((tm, LANES), lambda i: (i, 0)),
                   pl.BlockSpec((tm, LANES), lambda i: (i, 0)),
                   pl.BlockSpec((1, LANES), lambda i: (0, 0))],
        out_shape=[jax.ShapeDtypeStruct((n, d // 2), PACKED),
                   jax.ShapeDtypeStruct((n, LANES), jnp.int32),
                   jax.ShapeDtypeStruct((n, LANES), F32),
                   jax.ShapeDtypeStruct((1, LANES), jnp.int32)],
        scratch_shapes=[pltpu.VMEM((1, LANES), F32)],
        compiler_params=_params("arbitrary"),
        name="norm_router",
    )(h, gain.reshape(1, d), wr, br)


def _dispatch_plan(ids, counts):
    n = ids.shape[0]
    nka = n * 2
    nb = (nka + MOE_ROWS - 1) // MOE_ROWS + N_EXPERTS + GATHER_AHEAD
    e, rank = ids[:, :2], ids[:, 2:4]
    blocks = (counts + MOE_ROWS - 1) // MOE_ROWS
    blk_end = jnp.cumsum(blocks)
    blk_start = blk_end - blocks
    n_used = blk_end[-1]
    lanes = jnp.arange(N_EXPERTS, dtype=jnp.int32)
    row_start = jnp.sum(jnp.where(e[:, :, None] == lanes, blk_start * MOE_ROWS, 0), axis=-1)
    dest = (row_start + rank).astype(jnp.int32).reshape(nka)
    flat_tok = jnp.arange(nka, dtype=jnp.int32) // 2
    row_tok = jnp.zeros((nb * MOE_ROWS,), jnp.int32).at[dest].set(flat_tok)
    b = jnp.arange(nb, dtype=jnp.int32)
    block_e = jnp.minimum(jnp.sum(b[:, None] >= blk_end[None, :], axis=1), N_EXPERTS - 1).astype(jnp.int32)
    first = ((b == blk_start[block_e]) & (b < n_used)).astype(jnp.int32)
    nxt_blk = blk_end[block_e]
    nxt_e = jnp.where(nxt_blk < n_used, block_e[jnp.minimum(nxt_blk, nb - 1)], -1).astype(jnp.int32)
    return dest, row_tok, block_e, first, nxt_e, n_used.astype(jnp.int32).reshape(1), nb


def _row_copy(src_hbm, row, dst, dst_row, sem):
    return pltpu.make_async_copy(src_hbm.at[pl.ds(row, 1), :], dst.at[pl.ds(dst_row, 1), :], sem)


def _experts_body(be_ref, first_ref, nxt_ref, tok_ref, nu_ref, x_hbm, w1_hbm, w3_hbm, w2_hbm, o_ref,
                  xbuf, st13, st2, wb13, wb2, gsem, wsem, *, layer):
    i = pl.program_id(0)
    n_used = nu_ref[0]
    rows = xbuf.shape[1]
    de = st2.shape[0]
    n_slots = xbuf.shape[0]
    slot = lax.rem(i, n_slots)
    ahead_slot = lax.rem(i + GATHER_AHEAD, n_slots)

    def weight_copies(e):
        return (pltpu.make_async_copy(w1_hbm.at[layer, e], st13.at[0], wsem.at[0]),
                pltpu.make_async_copy(w3_hbm.at[layer, e], st13.at[1], wsem.at[1]),
                pltpu.make_async_copy(w2_hbm.at[layer, e], st2, wsem.at[2]))

    def start_gather(blk, dst_slot, lo=0, hi=rows):
        for r in range(lo, hi):
            _row_copy(x_hbm, tok_ref[blk * rows + r], xbuf.at[dst_slot], r, gsem.at[dst_slot]).start()

    @pl.when(i == 0)
    def _():
        for c in weight_copies(be_ref[0]):
            c.start(priority=WEIGHT_DMA_PRIORITY)
        for b in range(GATHER_AHEAD):
            start_gather(b, b)

    @pl.when(i < n_used + GATHER_AHEAD)
    def _():
        for r in range(rows):
            _row_copy(x_hbm, 0, xbuf.at[slot], r, gsem.at[slot]).wait()

    @pl.when(i < n_used)
    def _():
        @pl.when(first_ref[i] == 1)
        def _():
            for c in weight_copies(0):
                c.wait()
            wb13[:, :de] = st13[0].astype(BF16)
            wb13[:, de:] = st13[1].astype(BF16)
            wb2[...] = st2[...].astype(BF16)

            @pl.when(nxt_ref[i] >= 0)
            def _():
                for c in weight_copies(nxt_ref[i]):
                    c.start(priority=WEIGHT_DMA_PRIORITY)

        pieces = 8
        per = rows // pieces
        x = jnp.concatenate(_unpack_bf16_pair(xbuf[slot]), axis=1)
        half = rows // 2
        wide = (2 * de) // (2 * MXU_TILE) * (2 * MXU_TILE)
        start_gather(i + GATHER_AHEAD, ahead_slot, 0, per)
        h_a = jnp.dot(x, wb13[:, :wide], preferred_element_type=F32)
        start_gather(i + GATHER_AHEAD, ahead_slot, per, 2 * per)
        h_b = jnp.concatenate(
            [jnp.dot(x[:half], wb13[:, wide:], preferred_element_type=F32),
             jnp.dot(x[half:], wb13[:, wide:], preferred_element_type=F32)], axis=0)
        h13 = jnp.concatenate([h_a, h_b], axis=1)
        start_gather(i + GATHER_AHEAD, ahead_slot, 2 * per, 4 * per)
        h1, h3 = h13[:, :de], h13[:, de:]
        a = ((h1 / (1.0 + jnp.exp(-h1))) * h3).astype(BF16)
        d = o_ref.shape[1]
        cols = d // 4
        for c in range(4):
            o_ref[:, c * cols:(c + 1) * cols] = jnp.dot(a, wb2[:, c * cols:(c + 1) * cols],
                                                        preferred_element_type=F32)
            start_gather(i + GATHER_AHEAD, ahead_slot, (4 + c) * per, (5 + c) * per)

    @pl.when(i >= n_used)
    def _():
        o_ref[...] = jnp.zeros(o_ref.shape, o_ref.dtype)


def _experts(xn, row_tok, block_e, first, nxt_e, n_used, nb, w1, w3, w2, *, layer):
    d, de = w1.shape[2:]
    assert xn.shape[1] * 2 == d
    grid_spec = pltpu.PrefetchScalarGridSpec(
        num_scalar_prefetch=5,
        grid=(nb,),
        in_specs=[pl.BlockSpec(memory_space=pl.ANY)] * 4,
        out_specs=pl.BlockSpec((MOE_ROWS, d), lambda i, *_: (i, 0)),
        scratch_shapes=[pltpu.VMEM((GATHER_AHEAD + 1, MOE_ROWS, d // 2), PACKED),
                        pltpu.VMEM((2, d, de), F32),
                        pltpu.VMEM((de, d), F32),
                        pltpu.VMEM((d, 2 * de), BF16),
                        pltpu.VMEM((de, d), BF16),
                        pltpu.SemaphoreType.DMA((GATHER_AHEAD + 1,)),
                        pltpu.SemaphoreType.DMA((3,))],
    )
    return pl.pallas_call(
        functools.partial(_experts_body, layer=layer),
        grid_spec=grid_spec,
        out_shape=jax.ShapeDtypeStruct((nb * MOE_ROWS, d), F32),
        compiler_params=pltpu.CompilerParams(dimension_semantics=("arbitrary",),
                                             vmem_limit_bytes=EXPERTS_VMEM_LIMIT),
        name="experts",
    )(block_e, first, nxt_e, row_tok, n_used, xn, w1, w3, w2)


def _combine_body(pos_ref, h_ref, g_ref, gain_ref, y_hbm, *rest, last):
    if last:
        n_ref, ybuf, sem = rest
    else:
        o_ref, n_ref, ybuf, sem = rest
    i = pl.program_id(0)
    n_steps = pl.num_programs(0)
    tm = h_ref.shape[0]

    def start_gather(step, slot):
        for r in range(tm):
            base = (step * tm + r) * 2
            _row_copy(y_hbm, pos_ref[base], ybuf.at[slot, 0], r, sem.at[slot]).start()
            _row_copy(y_hbm, pos_ref[base + 1], ybuf.at[slot, 1], r, sem.at[slot]).start(priority=1)

    def wait_gather(slot):
        for r in range(tm):
            _row_copy(y_hbm, 0, ybuf.at[slot, 0], r, sem.at[slot]).wait()
            _row_copy(y_hbm, 0, ybuf.at[slot, 1], r, sem.at[slot]).wait()

    slot = i % 2

    @pl.when(i == 0)
    def _():
        start_gather(0, 0)

    @pl.when(i + 1 < n_steps)
    def _():
        start_gather(i + 1, 1 - slot)

    wait_gather(slot)
    g = g_ref[...]
    out = h_ref[...] + (g[:, 0:1] * ybuf[slot, 0] + g[:, 1:2] * ybuf[slot, 1])
    if not last:
        o_ref[...] = out
    ms = jnp.mean(out * out, axis=-1, keepdims=True)
    n_ref[...] = (out * lax.rsqrt(ms + RMS_EPS) * gain_ref[...]).astype(n_ref.dtype)


def _combine(h, gates, pos, yb, next_gain, *, last):
    n, d = h.shape
    tm = min(128, n)
    row_spec = pl.BlockSpec((tm, d), lambda i, pos_: (i, 0))
    grid_spec = pltpu.PrefetchScalarGridSpec(
        num_scalar_prefetch=1,
        grid=(n // tm,),
        in_specs=[row_spec,
                  pl.BlockSpec((tm, LANES), lambda i, pos_: (i, 0)),
                  pl.BlockSpec((1, d), lambda i, pos_: (0, 0)),
                  pl.BlockSpec(memory_space=pl.ANY)],
        out_specs=[row_spec] if last else [row_spec, row_spec],
        scratch_shapes=[pltpu.VMEM((2, 2, tm, d), F32),
                        pltpu.SemaphoreType.DMA((2,))],
    )
    normed = jax.ShapeDtypeStruct((n, d), F32 if last else BF16)
    return pl.pallas_call(
        functools.partial(_combine_body, last=last),
        grid_spec=grid_spec,
        out_shape=[normed] if last else [jax.ShapeDtypeStruct((n, d), F32), normed],
        compiler_params=_params("arbitrary"),
        name="moe_combine",
    )(pos, h, gates, next_gain.reshape(1, d), yb)


def _hierarchical_moe(h, gain, w_group, b_group, w_expert, b_expert, w1, w3, w2, next_gain, *, layer, last):
    xn, ids, gates, counts = _router(h, gain, w_group, b_group, w_expert, b_expert)
    dest, row_tok, block_e, first, nxt_e, n_used, nb = _dispatch_plan(ids[:, :4], counts[0, :N_EXPERTS])
    yb = _experts(xn, row_tok, block_e, first, nxt_e, n_used, nb, w1, w3, w2, layer=layer)
    return _combine(h, gates, dest, yb, next_gain, last=last)


def kernel(x, norm_mix, norm_ffn, norm_final, a_w_in, a_q_gain, a_k_gain, a_w_o, b_w_in, b_v_gain, b_w_s, b_bias, b_w_o, c_w_in, c_sink, c_w_o, d_w_in, d_rpb, d_w_o, moe_w_group, moe_b_group, moe_w_expert, moe_b_expert, moe_w1, moe_w3, moe_w2):
    batch, seq, d = x.shape
    h = x.reshape(batch * seq, d)
    ones = jnp.ones((HEAD_DIM,), F32)
    depth = norm_mix.shape[0]
    hn = _rmsnorm(h, norm_mix[0], BF16)
    for i in range(depth):
        m, j = i % N_MIXERS, i // N_MIXERS
        if m == 0:
            cos, sin = _axial_tables(seq)
            qkv = _matmul_qkv(hn, a_w_in[j], cos, sin, a_q_gain[j], a_k_gain[j],
                              mode="axial", seq=seq)
            mixed, w_o = _attn_dense(qkv, seq), a_w_o[j]
        elif m == 1:
            z = _matmul_gelu(hn, b_w_in[j], F32)
            mixed, w_o = _sgu(z, b_v_gain[j], b_w_s[j], b_bias[j]), b_w_o[j]
        elif m == 2:
            cos, sin = _rope_tables(seq)
            qkv = _matmul_qkv(hn, c_w_in[j], cos, sin, ones, ones, mode="rope", seq=seq)
            mixed, w_o = _attn_window(qkv, c_sink[j].astype(F32), seq), c_w_o[j]
        else:
            cos, sin = _rope_tables(seq)
            qkv = _matmul_qkv(hn, d_w_in[j], cos, sin, ones, ones, mode="none", seq=seq)
            mixed, w_o = _attn_nbr(qkv, d_rpb[j], seq), d_w_o[j]
        h = _matmul_residual(mixed, w_o, h)
        last = i == depth - 1
        outs = _hierarchical_moe(h, norm_ffn[i], moe_w_group[i], moe_b_group[i], moe_w_expert[i],
                                 moe_b_expert[i], moe_w1, moe_w3, moe_w2,
                                 norm_final if last else norm_mix[i + 1], layer=i, last=last)
        if last:
            return outs[0].reshape(batch, seq, d)
        h, hn = outs
```

```python
import functools

import jax
import jax.numpy as jnp
from jax import lax
from jax.experimental import pallas as pl
from jax.experimental.pallas import tpu as pltpu

D_MODEL = 4096
BATCH = 2
SEQ = 4096
DEPTH = 4
N_MIXERS = 4
HEAD_DIM = 128
GQA_GROUP = 4
ROPE_THETA = 10000.0
Q_BLOCK = 128
GRID_W = 64
SG_CHUNK = 128
SG_GROUPS = 8
WINDOW = 128
NA_ROWS = 8
NA_COLS = 16
N_GROUPS = 8
EXPERTS_PER_GROUP = 4
N_EXPERTS = N_GROUPS * EXPERTS_PER_GROUP
D_EXPERT = 384
MOE_ROWS = 256
GATHER_AHEAD = 2
RMS_EPS = 1e-6
NEG_INF = -1e30

LOG2_E = 1.4426950408889634
LANES = 128
MXU_TILE = 256
WEIGHT_DMA_PRIORITY = 1
DENSE_KEY_CHUNK = 1024
DENSE_Q_ROWS = 256
NA_WIN_ROWS = 10
WINDOW_Q_ROWS = 256
WINDOW_BLOCKS_PER_STEP = 4
NA_BLOCKS_PER_STEP = 4
VMEM_LIMIT = 48 * 1024 * 1024
EXPERTS_VMEM_LIMIT = 58 * 1024 * 1024

F32 = jnp.float32
BF16 = jnp.bfloat16
PACKED = jnp.uint32


def _params(*sem):
    return pltpu.CompilerParams(dimension_semantics=sem, vmem_limit_bytes=VMEM_LIMIT)


def _rmsnorm_body(x_ref, g_ref, o_ref):
    x = x_ref[...]
    ms = jnp.mean(x * x, axis=-1, keepdims=True)
    o_ref[...] = (x * lax.rsqrt(ms + RMS_EPS) * g_ref[...]).astype(o_ref.dtype)


def _rmsnorm(x, gain, out_dtype):
    n, d = x.shape
    tm = min(256, n)
    return pl.pallas_call(
        _rmsnorm_body,
        grid=(n // tm,),
        in_specs=[pl.BlockSpec((tm, d), lambda i: (i, 0)),
                  pl.BlockSpec((1, d), lambda i: (0, 0))],
        out_specs=pl.BlockSpec((tm, d), lambda i: (i, 0)),
        out_shape=jax.ShapeDtypeStruct((n, d), out_dtype),
        compiler_params=_params("parallel"),
        name="rmsnorm",
    )(x, gain.reshape(1, d))


def _gelu_tanh(x):
    return 0.5 * x * (1.0 + jnp.tanh(0.7978845608028654 * (x + 0.044715 * (x * x * x))))


def _mm_gelu_body(x_ref, w_ref, o_ref):
    acc = jnp.dot(x_ref[...], w_ref[...].astype(BF16), preferred_element_type=F32)
    o_ref[...] = _gelu_tanh(acc).astype(o_ref.dtype)


def _mm_res_body(x_ref, w_ref, r_ref, o_ref):
    acc = jnp.dot(x_ref[...], w_ref[...].astype(BF16), preferred_element_type=F32)
    o_ref[...] = r_ref[...] + acc


def _swap_halves(y, width):
    if 2 * width == LANES:
        return pltpu.roll(y, width, axis=1)
    lane = lax.broadcasted_iota(jnp.int32, y.shape, 1)
    first = (lane % (2 * width)) < width
    return jnp.where(first, pltpu.roll(y, LANES - width, axis=1), pltpu.roll(y, width, axis=1))


def _mm_qkv_body(x_ref, w_ref, cos_ref, sin_ref, gq_ref, gk_ref, o_ref, *, mode, n_q, n_qk, scale):
    j = pl.program_id(1)
    is_q = j < n_q
    is_qk = j < n_qk
    x = x_ref[...]
    tn = o_ref.shape[1]

    def head_post(a):
        if mode == "axial":
            gain = jnp.where(is_q, gq_ref[...] * scale, gk_ref[...])
            a = a * lax.rsqrt(jnp.mean(a * a, axis=-1, keepdims=True) + RMS_EPS)
        else:
            gain = jnp.where(is_q, jnp.full((1, HEAD_DIM), scale, F32), jnp.ones((1, HEAD_DIM), F32))
        a = a * gain
        if mode == "axial":
            a = a * cos_ref[...] + _swap_halves(a, HEAD_DIM // 4) * sin_ref[...]
        elif mode == "rope":
            a = a * cos_ref[...] + _swap_halves(a, HEAD_DIM // 2) * sin_ref[...]
        return a

    if mode == "axial":
        acc = jnp.dot(x, w_ref[...].astype(BF16), preferred_element_type=F32)

        @pl.when(is_qk)
        def _():
            for t in range(tn // HEAD_DIM):
                o_ref[:, t * HEAD_DIM:(t + 1) * HEAD_DIM] = head_post(
                    acc[:, t * HEAD_DIM:(t + 1) * HEAD_DIM]).astype(o_ref.dtype)

        @pl.when(jnp.logical_not(is_qk))
        def _():
            o_ref[...] = acc.astype(o_ref.dtype)
    else:
        piece = min(MXU_TILE, tn)
        for pc in range(tn // piece):
            acc = jnp.dot(x, w_ref[:, pc * piece:(pc + 1) * piece].astype(BF16), preferred_element_type=F32)
            for t in range(piece // HEAD_DIM):
                raw = acc[:, t * HEAD_DIM:(t + 1) * HEAD_DIM]
                lo = pc * piece + t * HEAD_DIM
                o_ref[:, lo:lo + HEAD_DIM] = jnp.where(is_qk, head_post(raw), raw).astype(o_ref.dtype)


def _mm_tiles(m, n):
    return min(1024, m), min(512, n)


def _matmul_gelu(x, w, out_dtype):
    m, k = x.shape
    n = w.shape[1]
    tm, tn = _mm_tiles(m, n)
    return pl.pallas_call(
        _mm_gelu_body,
        grid=(m // tm, n // tn),
        in_specs=[pl.BlockSpec((tm, k), lambda i, j: (i, 0)),
                  pl.BlockSpec((k, tn), lambda i, j: (0, j))],
        out_specs=pl.BlockSpec((tm, tn), lambda i, j: (i, j)),
        out_shape=jax.ShapeDtypeStruct((m, n), out_dtype),
        compiler_params=_params("parallel", "parallel"),
        name="matmul_gelu",
    )(x, w)


def _matmul_residual(x, w, res):
    m, k = x.shape
    n = w.shape[1]
    tm, tn = _mm_tiles(m, n)
    return pl.pallas_call(
        _mm_res_body,
        grid=(m // tm, n // tn),
        in_specs=[pl.BlockSpec((tm, k), lambda i, j: (i, 0)),
                  pl.BlockSpec((k, tn), lambda i, j: (0, j)),
                  pl.BlockSpec((tm, tn), lambda i, j: (i, j))],
        out_specs=pl.BlockSpec((tm, tn), lambda i, j: (i, j)),
        out_shape=jax.ShapeDtypeStruct((m, n), F32),
        compiler_params=_params("parallel", "parallel"),
        name="matmul_residual",
    )(x, w, res)


def _matmul_qkv(x, w, cos, sin, gq, gk, *, mode, seq):
    m, k = x.shape
    n = w.shape[1]
    d_q = D_MODEL
    d_kv = (n - d_q) // 2
    tm = min(1024, seq)
    tn = min(512, d_kv)
    s_blocks = seq // tm
    scale = HEAD_DIM ** -0.5 * (LOG2_E if mode == "axial" else 1.0)
    body = functools.partial(_mm_qkv_body, mode=mode, n_q=d_q // tn, n_qk=(d_q + d_kv) // tn, scale=scale)
    return pl.pallas_call(
        body,
        grid=(m // tm, n // tn),
        in_specs=[pl.BlockSpec((tm, k), lambda i, j: (i, 0)),
                  pl.BlockSpec((k, tn), lambda i, j: (0, j)),
                  pl.BlockSpec((tm, HEAD_DIM), lambda i, j: (i % s_blocks, 0)),
                  pl.BlockSpec((tm, HEAD_DIM), lambda i, j: (i % s_blocks, 0)),
                  pl.BlockSpec((1, HEAD_DIM), lambda i, j: (0, 0)),
                  pl.BlockSpec((1, HEAD_DIM), lambda i, j: (0, 0))],
        out_specs=pl.BlockSpec((tm, tn), lambda i, j: (i, j)),
        out_shape=jax.ShapeDtypeStruct((m, n), BF16),
        compiler_params=_params("parallel", "parallel"),
        name="matmul_qkv_" + mode,
    )(x, w, cos, sin, gq.reshape(1, HEAD_DIM), gk.reshape(1, HEAD_DIM))


def _rope_cos_sin(pos, dim):
    half = dim // 2
    inv = jnp.power(jnp.float32(ROPE_THETA), -jnp.arange(half, dtype=F32) * (2.0 / dim))
    ang = pos.astype(F32)[:, None] * inv[None, :]
    return jnp.cos(ang), jnp.sin(ang)


def _axial_tables(seq):
    t = jnp.arange(seq, dtype=jnp.int32)
    cr, sr = _rope_cos_sin(t // GRID_W, HEAD_DIM // 2)
    cc, sc = _rope_cos_sin(t % GRID_W, HEAD_DIM // 2)
    return (jnp.concatenate([cr, cr, cc, cc], axis=-1),
            jnp.concatenate([-sr, sr, -sc, sc], axis=-1))


def _rope_tables(seq):
    c, s = _rope_cos_sin(jnp.arange(seq, dtype=jnp.int32), HEAD_DIM)
    return jnp.concatenate([c, c], axis=-1), jnp.concatenate([-s, s], axis=-1)


def _stack_heads(q_ref, row0=0, rows=None):
    rows = q_ref.shape[0] if rows is None else rows
    return jnp.concatenate(
        [q_ref[row0:row0 + rows, g * HEAD_DIM:(g + 1) * HEAD_DIM] for g in range(GQA_GROUP)], axis=0)


def _unstack_heads(o, o_ref, row0=0, rows=None):
    rows = o_ref.shape[0] if rows is None else rows
    for g in range(GQA_GROUP):
        o_ref[row0:row0 + rows, g * HEAD_DIM:(g + 1) * HEAD_DIM] = o[g * rows:(g + 1) * rows].astype(o_ref.dtype)


def _qk(q, k):
    return lax.dot_general(q, k, (((1,), (1,)), ((), ())), preferred_element_type=F32)


def _attn_dense_body(q_ref, k_ref, v_ref, o_ref):
    q = _stack_heads(q_ref)
    seq = k_ref.shape[0]
    kc = min(DENSE_KEY_CHUNK, seq)
    m = l = acc = None
    for c in range(seq // kc):
        s = _qk(q, k_ref[c * kc:(c + 1) * kc, :])
        mc = jnp.max(s, axis=-1, keepdims=True)
        if c == 0:
            m = mc
            p = jnp.exp2(s - m)
            l = jnp.sum(p, axis=-1, keepdims=True)
            acc = jnp.dot(p.astype(BF16), v_ref[c * kc:(c + 1) * kc, :], preferred_element_type=F32)
        else:
            m_new = jnp.maximum(m, mc)
            alpha = jnp.exp2(m - m_new)
            p = jnp.exp2(s - m_new)
            l = alpha * l + jnp.sum(p, axis=-1, keepdims=True)
            acc = alpha * acc + jnp.dot(p.astype(BF16), v_ref[c * kc:(c + 1) * kc, :],
                                        preferred_element_type=F32)
            m = m_new
    _unstack_heads(acc / l, o_ref)


def _attn_dense(qkv, seq):
    n = qkv.shape[0]
    batch = n // seq
    n_kv = D_MODEL // (GQA_GROUP * HEAD_DIM)
    k_off = D_MODEL // HEAD_DIM
    v_off = k_off + n_kv
    tq = DENSE_Q_ROWS
    nq = seq // tq
    gw = GQA_GROUP * HEAD_DIM
    return pl.pallas_call(
        _attn_dense_body,
        grid=(batch, n_kv, nq),
        in_specs=[pl.BlockSpec((tq, gw), lambda b, h, i: (b * nq + i, h)),
                  pl.BlockSpec((seq, HEAD_DIM), lambda b, h, i: (b, k_off + h)),
                  pl.BlockSpec((seq, HEAD_DIM), lambda b, h, i: (b, v_off + h))],
        out_specs=pl.BlockSpec((tq, gw), lambda b, h, i: (b * nq + i, h)),
        out_shape=jax.ShapeDtypeStruct((n, D_MODEL), BF16),
        compiler_params=_params("parallel", "parallel", "parallel"),
        name="attn_dense",
    )(qkv, qkv, qkv)


def _attn_window_body(sink_ref, q_ref, k_ref, v_ref, o_ref):
    h = pl.program_id(1)
    tq = WINDOW_Q_ROWS
    nkeys = tq + 2 * WINDOW
    for sub in range(q_ref.shape[0] // tq):
        i = pl.program_id(2) * (q_ref.shape[0] // tq) + sub
        start = pl.multiple_of(jnp.clip(i * tq - WINDOW, 0, k_ref.shape[0] - nkeys), WINDOW)
        k3 = k_ref[pl.ds(start, nkeys), :]
        v3 = v_ref[pl.ds(start, nkeys), :]
        s = _qk(_stack_heads(q_ref, sub * tq, tq), k3)
        row = lax.broadcasted_iota(jnp.int32, s.shape, 0)
        col = lax.broadcasted_iota(jnp.int32, s.shape, 1)
        qpos = i * tq + row % tq
        kpos = start + col
        s = jnp.where(jnp.abs(qpos - kpos) <= WINDOW, s, NEG_INF)
        rowg = lax.broadcasted_iota(jnp.int32, (s.shape[0], 1), 0) // tq
        sink = jnp.zeros((s.shape[0], 1), F32)
        for g in range(GQA_GROUP):
            sink = jnp.where(rowg == g, sink_ref[h * GQA_GROUP + g], sink)
        m = jnp.maximum(jnp.max(s, axis=-1, keepdims=True), sink)
        p = jnp.exp(s - m)
        denom = jnp.sum(p, axis=-1, keepdims=True) + jnp.exp(sink - m)
        o = jnp.dot(p.astype(BF16), v3, preferred_element_type=F32) / denom
        _unstack_heads(o, o_ref, sub * tq, tq)


def _attn_window(qkv, sink, seq):
    n = qkv.shape[0]
    batch = n // seq
    n_kv = D_MODEL // (GQA_GROUP * HEAD_DIM)
    k_off = D_MODEL // HEAD_DIM
    v_off = k_off + n_kv
    tq = WINDOW_BLOCKS_PER_STEP * WINDOW_Q_ROWS
    nq = seq // tq
    gw = GQA_GROUP * HEAD_DIM
    assert seq >= WINDOW_Q_ROWS + 2 * WINDOW
    grid_spec = pltpu.PrefetchScalarGridSpec(
        num_scalar_prefetch=1,
        grid=(batch, n_kv, nq),
        in_specs=[pl.BlockSpec((tq, gw), lambda b, h, i, s_: (b * nq + i, h)),
                  pl.BlockSpec((seq, HEAD_DIM), lambda b, h, i, s_: (b, k_off + h)),
                  pl.BlockSpec((seq, HEAD_DIM), lambda b, h, i, s_: (b, v_off + h))],
        out_specs=pl.BlockSpec((tq, gw), lambda b, h, i, s_: (b * nq + i, h)),
    )
    return pl.pallas_call(
        _attn_window_body,
        grid_spec=grid_spec,
        out_shape=jax.ShapeDtypeStruct((n, D_MODEL), BF16),
        compiler_params=_params("parallel", "parallel", "parallel"),
        name="attn_window",
    )(sink, qkv, qkv, qkv)


def _na_window_start(i, rows):
    return jnp.clip(2 * i - NA_ROWS // 2, 0, rows - NA_WIN_ROWS)


NA_DR = 2 * NA_ROWS - 1
NA_DC = 2 * NA_COLS - 1
NA_PAIR_LO = -2
NA_PAIRS = NA_DR + 3


def _na_table_body(rpb_ref, o_ref):
    h = pl.program_id(0)
    shape = (GRID_W, 2 * GRID_W)
    c = lax.broadcasted_iota(jnp.int32, shape, 0)
    lane = lax.broadcasted_iota(jnp.int32, shape, 1)
    kc = lane % GRID_W
    d = kc - c + (NA_COLS - 1)
    cs = jnp.clip(c - NA_COLS // 2, 0, GRID_W - NA_COLS)
    col_ok = (kc >= cs) & (kc < cs + NA_COLS)
    neg = jnp.full(shape, NEG_INF, F32)
    rows = []
    for a in range(NA_DR):
        val = jnp.zeros(shape, F32)
        for b in range(NA_DC):
            val = jnp.where(d == b, rpb_ref[(h * NA_DR + a) * NA_DC + b], val)
        rows.append(jnp.where(col_ok, val, neg))
    for e in range(NA_PAIRS):
        a0 = e + NA_PAIR_LO
        lo = rows[a0] if 0 <= a0 < NA_DR else neg
        hi = rows[a0 + 1] if 0 <= a0 + 1 < NA_DR else neg
        o_ref[0, e] = jnp.where(lane >= GRID_W, hi, lo)


def _na_table(rpb):
    n_heads = rpb.shape[0]
    grid_spec = pltpu.PrefetchScalarGridSpec(
        num_scalar_prefetch=1,
        grid=(n_heads,),
        in_specs=[],
        out_specs=pl.BlockSpec((1, NA_PAIRS, GRID_W, 2 * GRID_W), lambda h, r: (h, 0, 0, 0)),
    )
    return pl.pallas_call(
        _na_table_body,
        grid_spec=grid_spec,
        out_shape=jax.ShapeDtypeStruct((n_heads, NA_PAIRS, GRID_W, 2 * GRID_W), F32),
        compiler_params=_params("arbitrary"),
        name="na_bias_table",
    )(rpb.astype(F32).reshape(-1))


def _attn_nbr_body(q_ref, k_ref, v_ref, t_ref, o_ref, *, rows):
    for sub in range(NA_BLOCKS_PER_STEP):
        i = pl.program_id(2) * NA_BLOCKS_PER_STEP + sub
        ws = _na_window_start(i, rows)
        start = pl.multiple_of(ws * GRID_W, GRID_W)
        nkeys = NA_WIN_ROWS * GRID_W
        kw = k_ref[pl.ds(start, nkeys), :]
        vw = v_ref[pl.ds(start, nkeys), :]
        lane = lax.broadcasted_iota(jnp.int32, (1, 2 * GRID_W), 1)
        blocks = [[] for _ in range(GQA_GROUP)]
        for u in range(Q_BLOCK // GRID_W):
            r = (Q_BLOCK // GRID_W) * i + u
            rs = jnp.clip(r - NA_ROWS // 2, 0, rows - NA_ROWS)
            tiles = [[] for _ in range(GQA_GROUP)]
            for jp in range(NA_WIN_ROWS // 2):
                kr = ws + 2 * jp
                ok0 = (kr >= rs) & (kr < rs + NA_ROWS)
                ok1 = (kr + 1 >= rs) & (kr + 1 < rs + NA_ROWS)
                pen = jnp.where(lane < GRID_W, jnp.where(ok0, 0.0, NEG_INF), jnp.where(ok1, 0.0, NEG_INF))
                e = kr - r + (NA_ROWS - 1) - NA_PAIR_LO
                for g in range(GQA_GROUP):
                    tiles[g].append(t_ref[g, e] + pen)
            for g in range(GQA_GROUP):
                blocks[g].append(jnp.concatenate(tiles[g], axis=1))
        bias = jnp.concatenate([b for g in range(GQA_GROUP) for b in blocks[g]], axis=0)
        s = _qk(_stack_heads(q_ref, sub * Q_BLOCK, Q_BLOCK), kw) + bias
        m = jnp.max(s, axis=-1, keepdims=True)
        p = jnp.exp(s - m)
        l = jnp.sum(p, axis=-1, keepdims=True)
        o = jnp.dot(p.astype(BF16), vw, preferred_element_type=F32) / l
        _unstack_heads(o, o_ref, sub * Q_BLOCK, Q_BLOCK)


def _attn_nbr(qkv, rpb, seq):
    n = qkv.shape[0]
    batch = n // seq
    n_kv = D_MODEL // (GQA_GROUP * HEAD_DIM)
    k_off = D_MODEL // HEAD_DIM
    v_off = k_off + n_kv
    tq = NA_BLOCKS_PER_STEP * Q_BLOCK
    nq = seq // tq
    gw = GQA_GROUP * HEAD_DIM
    rows = seq // GRID_W
    assert rows >= NA_WIN_ROWS and Q_BLOCK == 2 * GRID_W and rpb.shape[1:] == (NA_DR, NA_DC)
    table = _na_table(rpb)
    return pl.pallas_call(
        functools.partial(_attn_nbr_body, rows=rows),
        grid=(n_kv, batch, nq),
        in_specs=[pl.BlockSpec((tq, gw), lambda h, b, i: (b * nq + i, h)),
                  pl.BlockSpec((seq, HEAD_DIM), lambda h, b, i: (b, k_off + h)),
                  pl.BlockSpec((seq, HEAD_DIM), lambda h, b, i: (b, v_off + h)),
                  pl.BlockSpec((GQA_GROUP, NA_PAIRS, GRID_W, 2 * GRID_W), lambda h, b, i: (h, 0, 0, 0))],
        out_specs=pl.BlockSpec((tq, gw), lambda h, b, i: (b * nq + i, h)),
        out_shape=jax.ShapeDtypeStruct((n, D_MODEL), BF16),
        compiler_params=_params("parallel", "parallel", "parallel"),
        name="attn_nbr",
    )(qkv, qkv, qkv, table)


def _sgu_body(u_ref, v_ref, vg_ref, ws_ref, bs_ref, o_ref):
    v = v_ref[...]
    ms = jnp.mean(v * v, axis=-1, keepdims=True)
    vn = (v * lax.rsqrt(ms + RMS_EPS) * vg_ref[...]).astype(BF16)
    cg = v.shape[1] // SG_GROUPS
    for g in range(SG_GROUPS):
        mixed = jnp.dot(ws_ref[g], vn[:, g * cg:(g + 1) * cg], preferred_element_type=F32)
        mixed = mixed + bs_ref[:, g:g + 1]
        o_ref[:, g * cg:(g + 1) * cg] = (u_ref[:, g * cg:(g + 1) * cg] * mixed).astype(o_ref.dtype)


def _sgu(z, v_gain, w_s, b_s):
    n = z.shape[0]
    width = z.shape[1] // 2
    return pl.pallas_call(
        _sgu_body,
        grid=(n // SG_CHUNK,),
        in_specs=[pl.BlockSpec((SG_CHUNK, width), lambda c: (c, 0)),
                  pl.BlockSpec((SG_CHUNK, width), lambda c: (c, 1)),
                  pl.BlockSpec((1, width), lambda c: (0, 0)),
                  pl.BlockSpec((SG_GROUPS, SG_CHUNK, SG_CHUNK), lambda c: (0, 0, 0)),
                  pl.BlockSpec((SG_CHUNK, SG_GROUPS), lambda c: (0, 0))],
        out_specs=pl.BlockSpec((SG_CHUNK, width), lambda c: (c, 0)),
        out_shape=jax.ShapeDtypeStruct((n, width), BF16),
        compiler_params=_params("parallel"),
        name="sgu",
    )(z, z, v_gain.reshape(1, width), w_s.astype(BF16), b_s.T)


def _pack_bf16_pair(lo, hi):
    lo_bits = lax.bitcast_convert_type(lo.astype(BF16).astype(F32), PACKED)
    hi_bits = lax.bitcast_convert_type(hi.astype(BF16).astype(F32), PACKED)
    return (hi_bits & jnp.uint32(0xFFFF0000)) | (lo_bits >> 16)


def _unpack_bf16_pair_f32(words):
    lo = lax.bitcast_convert_type(words << 16, F32)
    hi = lax.bitcast_convert_type(words & jnp.uint32(0xFFFF0000), F32)
    return lo, hi


def _unpack_bf16_pair(words):
    lo, hi = _unpack_bf16_pair_f32(words)
    return lo.astype(BF16), hi.astype(BF16)


def _router_body(x_ref, g_ref, wr_ref, br_ref, xn_ref, ids_ref, gates_ref, counts_ref, carry):
    @pl.when(pl.program_id(0) == 0)
    def _():
        carry[...] = jnp.zeros(carry.shape, carry.dtype)

    x = x_ref[...]
    ms = jnp.mean(x * x, axis=-1, keepdims=True)
    xn = x * lax.rsqrt(ms + RMS_EPS) * g_ref[...]
    half_d = xn.shape[1] // 2
    xn_ref[...] = _pack_bf16_pair(xn[:, :half_d], xn[:, half_d:])
    xh = xn.astype(BF16)
    xl = (xn - xh.astype(F32)).astype(BF16)
    w = wr_ref[...]
    wh = w.astype(BF16)
    wl = (w - wh.astype(F32)).astype(BF16)
    hi = jnp.dot(xh, jnp.concatenate([wh, wl], axis=1), preferred_element_type=F32)
    lo = jnp.dot(xl, wh, preferred_element_type=F32)
    logits = (hi[:, :LANES] + (hi[:, LANES:] + lo)) + br_ref[...]
    lane = lax.broadcasted_iota(jnp.int32, logits.shape, 1)
    ninf = jnp.float32(-jnp.inf)

    lg = jnp.where(lane < N_GROUPS, logits, ninf)
    mg = jnp.max(lg, axis=-1, keepdims=True)
    grp = jnp.min(jnp.where(lg == mg, lane, LANES), axis=-1, keepdims=True)
    p_grp = 1.0 / jnp.sum(jnp.exp(lg - mg), axis=-1, keepdims=True)

    lo = N_GROUPS + grp * EXPERTS_PER_GROUP
    le = jnp.where((lane >= lo) & (lane < lo + EXPERTS_PER_GROUP), logits, ninf)
    m1 = jnp.max(le, axis=-1, keepdims=True)
    i1 = jnp.min(jnp.where(le == m1, lane, LANES), axis=-1, keepdims=True)
    le2 = jnp.where(lane == i1, ninf, le)
    m2 = jnp.max(le2, axis=-1, keepdims=True)
    i2 = jnp.min(jnp.where(le2 == m2, lane, LANES), axis=-1, keepdims=True)
    z = jnp.sum(jnp.exp(le - m1), axis=-1, keepdims=True)
    p1 = 1.0 / z
    p2 = jnp.exp(m2 - m1) / z
    tot = p1 + p2
    g1 = p_grp * (p1 / tot)
    g2 = p_grp * (p2 / tot)
    gates_ref[...] = jnp.where(lane == 0, g1, jnp.where(lane == 1, g2, 0.0))

    e0 = i1 - N_GROUPS
    e1 = i2 - N_GROUPS
    tm = x.shape[0]
    onehot = jnp.where((lane == e0) | (lane == e1), 1.0, 0.0)
    tri = jnp.where(lax.broadcasted_iota(jnp.int32, (tm, tm), 0) > lax.broadcasted_iota(jnp.int32, (tm, tm), 1),
                    1.0, 0.0).astype(BF16)
    before = jnp.dot(tri, onehot.astype(BF16), preferred_element_type=F32) + carry[...]
    r0 = jnp.sum(jnp.where(lane == e0, before, 0.0), axis=-1, keepdims=True).astype(jnp.int32)
    r1 = jnp.sum(jnp.where(lane == e1, before, 0.0), axis=-1, keepdims=True).astype(jnp.int32)
    carry[...] = carry[...] + jnp.sum(onehot, axis=0, keepdims=True)
    counts_ref[...] = carry[...].astype(jnp.int32)
    ids_ref[...] = jnp.where(lane == 0, e0, jnp.where(lane == 1, e1, jnp.where(lane == 2, r0, jnp.where(lane == 3, r1, 0))))


def _router(h, gain, w_group, b_group, w_expert, b_expert):
    n, d = h.shape
    tm = min(256, n)
    wr = jnp.concatenate([w_group, w_expert.transpose(1, 0, 2).reshape(d, N_EXPERTS)], axis=1)
    wr = jnp.pad(wr.astype(F32), ((0, 0), (0, LANES - wr.shape[1])))
    br = jnp.concatenate([b_group, b_expert.reshape(N_EXPERTS)]).astype(F32)
    br = jnp.pad(br, (0, LANES - br.shape[0])).reshape(1, LANES)
    return pl.pallas_call(
        _router_body,
        grid=(n // tm,),
        in_specs=[pl.BlockSpec((tm, d), lambda i: (i, 0)),
                  pl.BlockSpec((1, d), lambda i: (0, 0)),
                  pl.BlockSpec((d, LANES), lambda i: (0, 0)),
                  pl.BlockSpec((1, LANES), lambda i: (0, 0))],
        out_specs=[pl.BlockSpec((tm, d // 2), lambda i: (i, 0)),
                   pl.BlockSpec((tm, LANES), lambda i: (i, 0)),
                   pl.BlockSpec((tm, LANES), lambda i: (i, 0)),
                   pl.BlockSpec((1, LANES), lambda i: (0, 0))],
        out_shape=[jax.ShapeDtypeStruct((n, d // 2), PACKED),
                   jax.ShapeDtypeStruct((n, LANES), jnp.int32),
                   jax.ShapeDtypeStruct((n, LANES), F32),
                   jax.ShapeDtypeStruct((1, LANES), jnp.int32)],
        scratch_shapes=[pltpu.VMEM((1, LANES), F32)],
        compiler_params=_params("arbitrary"),
        name="norm_router",
    )(h, gain.reshape(1, d), wr, br)


def _dispatch_plan(ids, counts):
    n = ids.shape[0]
    nka = n * 2
    nb = (nka + MOE_ROWS - 1) // MOE_ROWS + N_EXPERTS + GATHER_AHEAD
    e, rank = ids[:, :2], ids[:, 2:4]
    blocks = (counts + MOE_ROWS - 1) // MOE_ROWS
    blk_end = jnp.cumsum(blocks)
    blk_start = blk_end - blocks
    n_used = blk_end[-1]
    lanes = jnp.arange(N_EXPERTS, dtype=jnp.int32)
    row_start = jnp.sum(jnp.where(e[:, :, None] == lanes, blk_start * MOE_ROWS, 0), axis=-1)
    dest = (row_start + rank).astype(jnp.int32).reshape(nka)
    flat_tok = jnp.arange(nka, dtype=jnp.int32) // 2
    row_tok = jnp.zeros((nb * MOE_ROWS,), jnp.int32).at[dest].set(flat_tok)
    b = jnp.arange(nb, dtype=jnp.int32)
    block_e = jnp.minimum(jnp.sum(b[:, None] >= blk_end[None, :], axis=1), N_EXPERTS - 1).astype(jnp.int32)
    first = ((b == blk_start[block_e]) & (b < n_used)).astype(jnp.int32)
    nxt_blk = blk_end[block_e]
    nxt_e = jnp.where(nxt_blk < n_used, block_e[jnp.minimum(nxt_blk, nb - 1)], -1).astype(jnp.int32)
    return dest, row_tok, block_e, first, nxt_e, n_used.astype(jnp.int32).reshape(1), nb


def _row_copy(src_hbm, row, dst, dst_row, sem):
    return pltpu.make_async_copy(src_hbm.at[pl.ds(row, 1), :], dst.at[pl.ds(dst_row, 1), :], sem)


def _experts_body(be_ref, first_ref, nxt_ref, tok_ref, nu_ref, x_hbm, w1_hbm, w3_hbm, w2_hbm, o_ref,
                  xbuf, st13, st2, wb13, wb2, gsem, wsem, *, layer):
    i = pl.program_id(0)
    n_used = nu_ref[0]
    rows = xbuf.shape[1]
    de = st2.shape[0]
    n_slots = xbuf.shape[0]
    slot = lax.rem(i, n_slots)
    ahead_slot = lax.rem(i + GATHER_AHEAD, n_slots)

    def weight_copies(e):
        return (pltpu.make_async_copy(w1_hbm.at[layer, e], st13.at[0], wsem.at[0]),
                pltpu.make_async_copy(w3_hbm.at[layer, e], st13.at[1], wsem.at[1]),
                pltpu.make_async_copy(w2_hbm.at[layer, e], st2, wsem.at[2]))

    def start_gather(blk, dst_slot, lo=0, hi=rows):
        for r in range(lo, hi):
            _row_copy(x_hbm, tok_ref[blk * rows + r], xbuf.at[dst_slot], r, gsem.at[dst_slot]).start()

    @pl.when(i == 0)
    def _():
        for c in weight_copies(be_ref[0]):
            c.start(priority=WEIGHT_DMA_PRIORITY)
        for b in range(GATHER_AHEAD):
            start_gather(b, b)

    @pl.when(i < n_used + GATHER_AHEAD)
    def _():
        for r in range(rows):
            _row_copy(x_hbm, 0, xbuf.at[slot], r, gsem.at[slot]).wait()

    @pl.when(i < n_used)
    def _():
        @pl.when(first_ref[i] == 1)
        def _():
            for c in weight_copies(0):
                c.wait()
            wb13[:, :de] = st13[0].astype(BF16)
            wb13[:, de:] = st13[1].astype(BF16)
            wb2[...] = st2[...].astype(BF16)

            @pl.when(nxt_ref[i] >= 0)
            def _():
                for c in weight_copies(nxt_ref[i]):
                    c.start(priority=WEIGHT_DMA_PRIORITY)

        pieces = 8
        per = rows // pieces
        x = jnp.concatenate(_unpack_bf16_pair(xbuf[slot]), axis=1)
        half = rows // 2
        wide = (2 * de) // (2 * MXU_TILE) * (2 * MXU_TILE)
        start_gather(i + GATHER_AHEAD, ahead_slot, 0, per)
        h_a = jnp.dot(x, wb13[:, :wide], preferred_element_type=F32)
        start_gather(i + GATHER_AHEAD, ahead_slot, per, 2 * per)
        h_b = jnp.concatenate(
            [jnp.dot(x[:half], wb13[:, wide:], preferred_element_type=F32),
             jnp.dot(x[half:], wb13[:, wide:], preferred_element_type=F32)], axis=0)
        h13 = jnp.concatenate([h_a, h_b], axis=1)
        start_gather(i + GATHER_AHEAD, ahead_slot, 2 * per, 4 * per)
        h1, h3 = h13[:, :de], h13[:, de:]
        a = ((h1 / (1.0 + jnp.exp(-h1))) * h3).astype(BF16)
        half_d = wb2.shape[1] // 2
        cols = half_d // 2
        for c in range(2):
            lo = jnp.dot(a, wb2[:, c * cols:(c + 1) * cols], preferred_element_type=F32)
            start_gather(i + GATHER_AHEAD, ahead_slot, (4 + 2 * c) * per, (5 + 2 * c) * per)
            hi = jnp.dot(a, wb2[:, half_d + c * cols:half_d + (c + 1) * cols], preferred_element_type=F32)
            start_gather(i + GATHER_AHEAD, ahead_slot, (5 + 2 * c) * per, (6 + 2 * c) * per)
            o_ref[:, c * cols:(c + 1) * cols] = _pack_bf16_pair(lo, hi)

    @pl.when(i >= n_used)
    def _():
        o_ref[...] = jnp.zeros(o_ref.shape, o_ref.dtype)


def _experts(xn, row_tok, block_e, first, nxt_e, n_used, nb, w1, w3, w2, *, layer):
    d, de = w1.shape[2:]
    assert xn.shape[1] * 2 == d
    grid_spec = pltpu.PrefetchScalarGridSpec(
        num_scalar_prefetch=5,
        grid=(nb,),
        in_specs=[pl.BlockSpec(memory_space=pl.ANY)] * 4,
        out_specs=pl.BlockSpec((MOE_ROWS, d // 2), lambda i, *_: (i, 0)),
        scratch_shapes=[pltpu.VMEM((GATHER_AHEAD + 1, MOE_ROWS, d // 2), PACKED),
                        pltpu.VMEM((2, d, de), F32),
                        pltpu.VMEM((de, d), F32),
                        pltpu.VMEM((d, 2 * de), BF16),
                        pltpu.VMEM((de, d), BF16),
                        pltpu.SemaphoreType.DMA((GATHER_AHEAD + 1,)),
                        pltpu.SemaphoreType.DMA((3,))],
    )
    return pl.pallas_call(
        functools.partial(_experts_body, layer=layer),
        grid_spec=grid_spec,
        out_shape=jax.ShapeDtypeStruct((nb * MOE_ROWS, d // 2), PACKED),
        compiler_params=pltpu.CompilerParams(dimension_semantics=("arbitrary",),
                                             vmem_limit_bytes=EXPERTS_VMEM_LIMIT),
        name="experts",
    )(block_e, first, nxt_e, row_tok, n_used, xn, w1, w3, w2)


def _combine_body(pos_ref, h_ref, g_ref, gain_ref, y_hbm, *rest, last):
    if last:
        n_ref, ybuf, sem = rest
    else:
        o_ref, n_ref, ybuf, sem = rest
    i = pl.program_id(0)
    n_steps = pl.num_programs(0)
    tm = h_ref.shape[0]

    def start_gather(step, slot):
        for r in range(tm):
            base = (step * tm + r) * 2
            _row_copy(y_hbm, pos_ref[base], ybuf.at[slot, 0], r, sem.at[slot]).start()
            _row_copy(y_hbm, pos_ref[base + 1], ybuf.at[slot, 1], r, sem.at[slot]).start()

    def wait_gather(slot):
        for r in range(tm):
            _row_copy(y_hbm, 0, ybuf.at[slot, 0], r, sem.at[slot]).wait()
            _row_copy(y_hbm, 0, ybuf.at[slot, 1], r, sem.at[slot]).wait()

    slot = i % 2

    @pl.when(i == 0)
    def _():
        start_gather(0, 0)

    @pl.when(i + 1 < n_steps)
    def _():
        start_gather(i + 1, 1 - slot)

    wait_gather(slot)
    g = g_ref[...]
    lo0, hi0 = _unpack_bf16_pair_f32(ybuf[slot, 0])
    lo1, hi1 = _unpack_bf16_pair_f32(ybuf[slot, 1])
    moe = jnp.concatenate([g[:, 0:1] * lo0 + g[:, 1:2] * lo1, g[:, 0:1] * hi0 + g[:, 1:2] * hi1], axis=1)
    out = h_ref[...] + moe
    if not last:
        o_ref[...] = out
    ms = jnp.mean(out * out, axis=-1, keepdims=True)
    n_ref[...] = (out * lax.rsqrt(ms + RMS_EPS) * gain_ref[...]).astype(n_ref.dtype)


def _combine(h, gates, pos, yb, next_gain, *, last):
    n, d = h.shape
    tm = min(128, n)
    row_spec = pl.BlockSpec((tm, d), lambda i, pos_: (i, 0))
    grid_spec = pltpu.PrefetchScalarGridSpec(
        num_scalar_prefetch=1,
        grid=(n // tm,),
        in_specs=[row_spec,
                  pl.BlockSpec((tm, LANES), lambda i, pos_: (i, 0)),
                  pl.BlockSpec((1, d), lambda i, pos_: (0, 0)),
                  pl.BlockSpec(memory_space=pl.ANY)],
        out_specs=[row_spec] if last else [row_spec, row_spec],
        scratch_shapes=[pltpu.VMEM((2, 2, tm, d // 2), PACKED),
                        pltpu.SemaphoreType.DMA((2,))],
    )
    normed = jax.ShapeDtypeStruct((n, d), F32 if last else BF16)
    return pl.pallas_call(
        functools.partial(_combine_body, last=last),
        grid_spec=grid_spec,
        out_shape=[normed] if last else [jax.ShapeDtypeStruct((n, d), F32), normed],
        compiler_params=_params("arbitrary"),
        name="moe_combine",
    )(pos, h, gates, next_gain.reshape(1, d), yb)


def _hierarchical_moe(h, gain, w_group, b_group, w_expert, b_expert, w1, w3, w2, next_gain, *, layer, last):
    xn, ids, gates, counts = _router(h, gain, w_group, b_group, w_expert, b_expert)
    dest, row_tok, block_e, first, nxt_e, n_used, nb = _dispatch_plan(ids[:, :4], counts[0, :N_EXPERTS])
    yb = _experts(xn, row_tok, block_e, first, nxt_e, n_used, nb, w1, w3, w2, layer=layer)
    return _combine(h, gates, dest, yb, next_gain, last=last)


def kernel(x, norm_mix, norm_ffn, norm_final, a_w_in, a_q_gain, a_k_gain, a_w_o, b_w_in, b_v_gain, b_w_s, b_bias, b_w_o, c_w_in, c_sink, c_w_o, d_w_in, d_rpb, d_w_o, moe_w_group, moe_b_group, moe_w_expert, moe_b_expert, moe_w1, moe_w3, moe_w2):
    batch, seq, d = x.shape
    h = x.reshape(batch * seq, d)
    ones = jnp.ones((HEAD_DIM,), F32)
    depth = norm_mix.shape[0]
    hn = _rmsnorm(h, norm_mix[0], BF16)
    for i in range(depth):
        m, j = i % N_MIXERS, i // N_MIXERS
        if m == 0:
            cos, sin = _axial_tables(seq)
            qkv = _matmul_qkv(hn, a_w_in[j], cos, sin, a_q_gain[j], a_k_gain[j],
                              mode="axial", seq=seq)
            mixed, w_o = _attn_dense(qkv, seq), a_w_o[j]
        elif m == 1:
            z = _matmul_gelu(hn, b_w_in[j], F32)
            mixed, w_o = _sgu(z, b_v_gain[j], b_w_s[j], b_bias[j]), b_w_o[j]
        elif m == 2:
            cos, sin = _rope_tables(seq)
            qkv = _matmul_qkv(hn, c_w_in[j], cos, sin, ones, ones, mode="rope", seq=seq)
            mixed, w_o = _attn_window(qkv, c_sink[j].astype(F32), seq), c_w_o[j]
        else:
            cos, sin = _rope_tables(seq)
            qkv = _matmul_qkv(hn, d_w_in[j], cos, sin, ones, ones, mode="none", seq=seq)
            mixed, w_o = _attn_nbr(qkv, d_rpb[j], seq), d_w_o[j]
        h = _matmul_residual(mixed, w_o, h)
        last = i == depth - 1
        outs = _hierarchical_moe(h, norm_ffn[i], moe_w_group[i], moe_b_group[i], moe_w_expert[i],
                                 moe_b_expert[i], moe_w1, moe_w3, moe_w2,
                                 norm_final if last else norm_mix[i + 1], layer=i, last=last)
        if last:
            return outs[0].reshape(batch, seq, d)
        h, hn = outs
```

```python
import functools

import jax
import jax.numpy as jnp
from jax import lax
from jax.experimental import pallas as pl
from jax.experimental.pallas import tpu as pltpu

D_MODEL = 4096
BATCH = 2
SEQ = 4096
DEPTH = 4
N_MIXERS = 4
HEAD_DIM = 128
GQA_GROUP = 4
ROPE_THETA = 10000.0
Q_BLOCK = 128
GRID_W = 64
SG_CHUNK = 128
SG_GROUPS = 8
WINDOW = 128
NA_ROWS = 8
NA_COLS = 16
N_GROUPS = 8
EXPERTS_PER_GROUP = 4
N_EXPERTS = N_GROUPS * EXPERTS_PER_GROUP
D_EXPERT = 384
MOE_ROWS = 256
GATHER_AHEAD = 2
RMS_EPS = 1e-6
NEG_INF = -1e30

LOG2_E = 1.4426950408889634
LANES = 128
MXU_TILE = 256
WEIGHT_DMA_PRIORITY = 1
DENSE_KEY_CHUNK = 1024
DENSE_Q_ROWS = 256
NA_WIN_ROWS = 10
WINDOW_Q_ROWS = 256
WINDOW_BLOCKS_PER_STEP = 8
NA_BLOCKS_PER_STEP = 8
VMEM_LIMIT = 48 * 1024 * 1024
EXPERTS_VMEM_LIMIT = 58 * 1024 * 1024

F32 = jnp.float32
BF16 = jnp.bfloat16
PACKED = jnp.uint32


def _params(*sem):
    return pltpu.CompilerParams(dimension_semantics=sem, vmem_limit_bytes=VMEM_LIMIT)


def _rmsnorm_body(x_ref, g_ref, o_ref):
    x = x_ref[...]
    ms = jnp.mean(x * x, axis=-1, keepdims=True)
    o_ref[...] = (x * lax.rsqrt(ms + RMS_EPS) * g_ref[...]).astype(o_ref.dtype)


def _rmsnorm(x, gain, out_dtype):
    n, d = x.shape
    tm = min(256, n)
    return pl.pallas_call(
        _rmsnorm_body,
        grid=(n // tm,),
        in_specs=[pl.BlockSpec((tm, d), lambda i: (i, 0)),
                  pl.BlockSpec((1, d), lambda i: (0, 0))],
        out_specs=pl.BlockSpec((tm, d), lambda i: (i, 0)),
        out_shape=jax.ShapeDtypeStruct((n, d), out_dtype),
        compiler_params=_params("parallel"),
        name="rmsnorm",
    )(x, gain.reshape(1, d))


def _gelu_tanh(x):
    return 0.5 * x * (1.0 + jnp.tanh(0.7978845608028654 * (x + 0.044715 * (x * x * x))))


def _mm_gelu_body(x_ref, w_ref, o_ref):
    acc = jnp.dot(x_ref[...], w_ref[...].astype(BF16), preferred_element_type=F32)
    o_ref[...] = _gelu_tanh(acc).astype(o_ref.dtype)


def _mm_res_body(x_ref, w_ref, r_ref, o_ref):
    acc = jnp.dot(x_ref[...], w_ref[...].astype(BF16), preferred_element_type=F32)
    o_ref[...] = r_ref[...] + acc


def _swap_halves(y, width):
    if 2 * width == LANES:
        return pltpu.roll(y, width, axis=1)
    lane = lax.broadcasted_iota(jnp.int32, y.shape, 1)
    first = (lane % (2 * width)) < width
    return jnp.where(first, pltpu.roll(y, LANES - width, axis=1), pltpu.roll(y, width, axis=1))


def _mm_qkv_body(x_ref, w_ref, cos_ref, sin_ref, gq_ref, gk_ref, o_ref, *, mode, n_q, n_qk, scale):
    j = pl.program_id(1)
    is_q = j < n_q
    is_qk = j < n_qk
    x = x_ref[...]
    tn = o_ref.shape[1]

    def head_post(a):
        if mode == "axial":
            gain = jnp.where(is_q, gq_ref[...] * scale, gk_ref[...])
            a = a * lax.rsqrt(jnp.mean(a * a, axis=-1, keepdims=True) + RMS_EPS)
        else:
            gain = jnp.where(is_q, jnp.full((1, HEAD_DIM), scale, F32), jnp.ones((1, HEAD_DIM), F32))
        a = a * gain
        if mode == "axial":
            a = a * cos_ref[...] + _swap_halves(a, HEAD_DIM // 4) * sin_ref[...]
        elif mode == "rope":
            a = a * cos_ref[...] + _swap_halves(a, HEAD_DIM // 2) * sin_ref[...]
        return a

    if mode == "axial":
        acc = jnp.dot(x, w_ref[...].astype(BF16), preferred_element_type=F32)

        @pl.when(is_qk)
        def _():
            for t in range(tn // HEAD_DIM):
                o_ref[:, t * HEAD_DIM:(t + 1) * HEAD_DIM] = head_post(
                    acc[:, t * HEAD_DIM:(t + 1) * HEAD_DIM]).astype(o_ref.dtype)

        @pl.when(jnp.logical_not(is_qk))
        def _():
            o_ref[...] = acc.astype(o_ref.dtype)
    else:
        piece = min(MXU_TILE, tn)
        for pc in range(tn // piece):
            acc = jnp.dot(x, w_ref[:, pc * piece:(pc + 1) * piece].astype(BF16), preferred_element_type=F32)
            for t in range(piece // HEAD_DIM):
                raw = acc[:, t * HEAD_DIM:(t + 1) * HEAD_DIM]
                lo = pc * piece + t * HEAD_DIM
                o_ref[:, lo:lo + HEAD_DIM] = jnp.where(is_qk, head_post(raw), raw).astype(o_ref.dtype)


def _mm_tiles(m, n):
    return min(1024, m), min(512, n)


def _matmul_gelu(x, w, out_dtype):
    m, k = x.shape
    n = w.shape[1]
    tm, tn = _mm_tiles(m, n)
    return pl.pallas_call(
        _mm_gelu_body,
        grid=(m // tm, n // tn),
        in_specs=[pl.BlockSpec((tm, k), lambda i, j: (i, 0)),
                  pl.BlockSpec((k, tn), lambda i, j: (0, j))],
        out_specs=pl.BlockSpec((tm, tn), lambda i, j: (i, j)),
        out_shape=jax.ShapeDtypeStruct((m, n), out_dtype),
        compiler_params=_params("parallel", "parallel"),
        name="matmul_gelu",
    )(x, w)


def _matmul_residual(x, w, res):
    m, k = x.shape
    n = w.shape[1]
    tm, tn = _mm_tiles(m, n)
    return pl.pallas_call(
        _mm_res_body,
        grid=(m // tm, n // tn),
        in_specs=[pl.BlockSpec((tm, k), lambda i, j: (i, 0)),
                  pl.BlockSpec((k, tn), lambda i, j: (0, j)),
                  pl.BlockSpec((tm, tn), lambda i, j: (i, j))],
        out_specs=pl.BlockSpec((tm, tn), lambda i, j: (i, j)),
        out_shape=jax.ShapeDtypeStruct((m, n), F32),
        compiler_params=_params("parallel", "parallel"),
        name="matmul_residual",
    )(x, w, res)


def _matmul_qkv(x, w, cos, sin, gq, gk, *, mode, seq):
    m, k = x.shape
    n = w.shape[1]
    d_q = D_MODEL
    d_kv = (n - d_q) // 2
    tm = min(1024, seq)
    tn = min(512, d_kv)
    s_blocks = seq // tm
    scale = HEAD_DIM ** -0.5 * LOG2_E
    body = functools.partial(_mm_qkv_body, mode=mode, n_q=d_q // tn, n_qk=(d_q + d_kv) // tn, scale=scale)
    return pl.pallas_call(
        body,
        grid=(m // tm, n // tn),
        in_specs=[pl.BlockSpec((tm, k), lambda i, j: (i, 0)),
                  pl.BlockSpec((k, tn), lambda i, j: (0, j)),
                  pl.BlockSpec((tm, HEAD_DIM), lambda i, j: (i % s_blocks, 0)),
                  pl.BlockSpec((tm, HEAD_DIM), lambda i, j: (i % s_blocks, 0)),
                  pl.BlockSpec((1, HEAD_DIM), lambda i, j: (0, 0)),
                  pl.BlockSpec((1, HEAD_DIM), lambda i, j: (0, 0))],
        out_specs=pl.BlockSpec((tm, tn), lambda i, j: (i, j)),
        out_shape=jax.ShapeDtypeStruct((m, n), BF16),
        compiler_params=_params("parallel", "parallel"),
        name="matmul_qkv_" + mode,
    )(x, w, cos, sin, gq.reshape(1, HEAD_DIM), gk.reshape(1, HEAD_DIM))


def _rope_cos_sin(pos, dim):
    half = dim // 2
    inv = jnp.power(jnp.float32(ROPE_THETA), -jnp.arange(half, dtype=F32) * (2.0 / dim))
    ang = pos.astype(F32)[:, None] * inv[None, :]
    return jnp.cos(ang), jnp.sin(ang)


def _axial_tables(seq):
    t = jnp.arange(seq, dtype=jnp.int32)
    cr, sr = _rope_cos_sin(t // GRID_W, HEAD_DIM // 2)
    cc, sc = _rope_cos_sin(t % GRID_W, HEAD_DIM // 2)
    return (jnp.concatenate([cr, cr, cc, cc], axis=-1),
            jnp.concatenate([-sr, sr, -sc, sc], axis=-1))


def _rope_tables(seq):
    c, s = _rope_cos_sin(jnp.arange(seq, dtype=jnp.int32), HEAD_DIM)
    return jnp.concatenate([c, c], axis=-1), jnp.concatenate([-s, s], axis=-1)


def _stack_heads(q_ref, row0=0, rows=None):
    rows = q_ref.shape[0] if rows is None else rows
    return jnp.concatenate(
        [q_ref[row0:row0 + rows, g * HEAD_DIM:(g + 1) * HEAD_DIM] for g in range(GQA_GROUP)], axis=0)


def _unstack_heads(o, o_ref, row0=0, rows=None):
    rows = o_ref.shape[0] if rows is None else rows
    for g in range(GQA_GROUP):
        o_ref[row0:row0 + rows, g * HEAD_DIM:(g + 1) * HEAD_DIM] = o[g * rows:(g + 1) * rows].astype(o_ref.dtype)


def _qk(q, k):
    return lax.dot_general(q, k, (((1,), (1,)), ((), ())), preferred_element_type=F32)


def _attn_dense_body(q_ref, k_ref, v_ref, o_ref):
    q = _stack_heads(q_ref)
    seq = k_ref.shape[0]
    kc = min(DENSE_KEY_CHUNK, seq)
    m = l = acc = None
    for c in range(seq // kc):
        s = _qk(q, k_ref[c * kc:(c + 1) * kc, :])
        mc = jnp.max(s, axis=-1, keepdims=True)
        if c == 0:
            m = mc
            p = jnp.exp2(s - m)
            l = jnp.sum(p, axis=-1, keepdims=True)
            acc = jnp.dot(p.astype(BF16), v_ref[c * kc:(c + 1) * kc, :], preferred_element_type=F32)
        else:
            m_new = jnp.maximum(m, mc)
            alpha = jnp.exp2(m - m_new)
            p = jnp.exp2(s - m_new)
            l = alpha * l + jnp.sum(p, axis=-1, keepdims=True)
            acc = alpha * acc + jnp.dot(p.astype(BF16), v_ref[c * kc:(c + 1) * kc, :],
                                        preferred_element_type=F32)
            m = m_new
    _unstack_heads(acc / l, o_ref)


def _attn_dense(qkv, seq):
    n = qkv.shape[0]
    batch = n // seq
    n_kv = D_MODEL // (GQA_GROUP * HEAD_DIM)
    k_off = D_MODEL // HEAD_DIM
    v_off = k_off + n_kv
    tq = DENSE_Q_ROWS
    nq = seq // tq
    gw = GQA_GROUP * HEAD_DIM
    return pl.pallas_call(
        _attn_dense_body,
        grid=(batch, n_kv, nq),
        in_specs=[pl.BlockSpec((tq, gw), lambda b, h, i: (b * nq + i, h)),
                  pl.BlockSpec((seq, HEAD_DIM), lambda b, h, i: (b, k_off + h)),
                  pl.BlockSpec((seq, HEAD_DIM), lambda b, h, i: (b, v_off + h))],
        out_specs=pl.BlockSpec((tq, gw), lambda b, h, i: (b * nq + i, h)),
        out_shape=jax.ShapeDtypeStruct((n, D_MODEL), BF16),
        compiler_params=_params("parallel", "parallel", "parallel"),
        name="attn_dense",
    )(qkv, qkv, qkv)


def _attn_window_body(sink_ref, q_ref, k_ref, v_ref, o_ref):
    h = pl.program_id(1)
    tq = WINDOW_Q_ROWS
    nkeys = tq + 2 * WINDOW
    shape = (GQA_GROUP * tq, nkeys)
    rel = (lax.broadcasted_iota(jnp.int32, shape, 0) % tq
           - lax.broadcasted_iota(jnp.int32, shape, 1) + WINDOW)
    rowg = lax.broadcasted_iota(jnp.int32, (shape[0], 1), 0) // tq
    sink = jnp.zeros((shape[0], 1), F32)
    for g in range(GQA_GROUP):
        sink = jnp.where(rowg == g, sink_ref[h * GQA_GROUP + g] * LOG2_E, sink)
    for sub in range(q_ref.shape[0] // tq):
        i = pl.program_id(2) * (q_ref.shape[0] // tq) + sub
        start = pl.multiple_of(jnp.clip(i * tq - WINDOW, 0, k_ref.shape[0] - nkeys), WINDOW)
        k3 = k_ref[pl.ds(start, nkeys), :]
        v3 = v_ref[pl.ds(start, nkeys), :]
        s = _qk(_stack_heads(q_ref, sub * tq, tq), k3)
        in_window = (rel + (i * tq - start)).astype(jnp.uint32) <= 2 * WINDOW
        s = jnp.where(in_window, s, NEG_INF)
        m = jnp.maximum(jnp.max(s, axis=-1, keepdims=True), sink)
        p = jnp.exp2(s - m)
        denom = jnp.sum(p, axis=-1, keepdims=True) + jnp.exp2(sink - m)
        o = jnp.dot(p.astype(BF16), v3, preferred_element_type=F32) / denom
        _unstack_heads(o, o_ref, sub * tq, tq)


def _attn_window(qkv, sink, seq):
    n = qkv.shape[0]
    batch = n // seq
    n_kv = D_MODEL // (GQA_GROUP * HEAD_DIM)
    k_off = D_MODEL // HEAD_DIM
    v_off = k_off + n_kv
    tq = min(WINDOW_BLOCKS_PER_STEP * WINDOW_Q_ROWS, seq)
    nq = seq // tq
    gw = GQA_GROUP * HEAD_DIM
    assert seq >= WINDOW_Q_ROWS + 2 * WINDOW
    grid_spec = pltpu.PrefetchScalarGridSpec(
        num_scalar_prefetch=1,
        grid=(batch, n_kv, nq),
        in_specs=[pl.BlockSpec((tq, gw), lambda b, h, i, s_: (b * nq + i, h)),
                  pl.BlockSpec((seq, HEAD_DIM), lambda b, h, i, s_: (b, k_off + h)),
                  pl.BlockSpec((seq, HEAD_DIM), lambda b, h, i, s_: (b, v_off + h))],
        out_specs=pl.BlockSpec((tq, gw), lambda b, h, i, s_: (b * nq + i, h)),
    )
    return pl.pallas_call(
        _attn_window_body,
        grid_spec=grid_spec,
        out_shape=jax.ShapeDtypeStruct((n, D_MODEL), BF16),
        compiler_params=_params("parallel", "parallel", "parallel"),
        name="attn_window",
    )(sink, qkv, qkv, qkv)


def _na_window_start(i, rows):
    return jnp.clip(2 * i - NA_ROWS // 2, 0, rows - NA_WIN_ROWS)


NA_DR = 2 * NA_ROWS - 1
NA_DC = 2 * NA_COLS - 1
NA_PAIR_LO = -2
NA_PAIRS = NA_DR + 3


def _na_table_body(rpb_ref, o_ref):
    h = pl.program_id(0)
    shape = (GRID_W, 2 * GRID_W)
    c = lax.broadcasted_iota(jnp.int32, shape, 0)
    lane = lax.broadcasted_iota(jnp.int32, shape, 1)
    kc = lane % GRID_W
    d = kc - c + (NA_COLS - 1)
    cs = jnp.clip(c - NA_COLS // 2, 0, GRID_W - NA_COLS)
    col_ok = (kc >= cs) & (kc < cs + NA_COLS)
    neg = jnp.full(shape, NEG_INF, F32)
    rows = []
    for a in range(NA_DR):
        val = jnp.zeros(shape, F32)
        for b in range(NA_DC):
            val = jnp.where(d == b, rpb_ref[(h * NA_DR + a) * NA_DC + b] * LOG2_E, val)
        rows.append(jnp.where(col_ok, val, neg))
    for e in range(NA_PAIRS):
        a0 = e + NA_PAIR_LO
        lo = rows[a0] if 0 <= a0 < NA_DR else neg
        hi = rows[a0 + 1] if 0 <= a0 + 1 < NA_DR else neg
        o_ref[0, e] = jnp.where(lane >= GRID_W, hi, lo)


def _na_table(rpb):
    n_heads = rpb.shape[0]
    grid_spec = pltpu.PrefetchScalarGridSpec(
        num_scalar_prefetch=1,
        grid=(n_heads,),
        in_specs=[],
        out_specs=pl.BlockSpec((1, NA_PAIRS, GRID_W, 2 * GRID_W), lambda h, r: (h, 0, 0, 0)),
    )
    return pl.pallas_call(
        _na_table_body,
        grid_spec=grid_spec,
        out_shape=jax.ShapeDtypeStruct((n_heads, NA_PAIRS, GRID_W, 2 * GRID_W), F32),
        compiler_params=_params("arbitrary"),
        name="na_bias_table",
    )(rpb.astype(F32).reshape(-1))


def _attn_nbr_body(q_ref, k_ref, v_ref, t_ref, o_ref, *, rows):
    n_sub = q_ref.shape[0] // Q_BLOCK
    for sub in range(n_sub):
        i = pl.program_id(2) * n_sub + sub
        ws = _na_window_start(i, rows)
        start = pl.multiple_of(ws * GRID_W, GRID_W)
        nkeys = NA_WIN_ROWS * GRID_W
        kw = k_ref[pl.ds(start, nkeys), :]
        vw = v_ref[pl.ds(start, nkeys), :]
        lane = lax.broadcasted_iota(jnp.int32, (1, 2 * GRID_W), 1)
        blocks = [[] for _ in range(GQA_GROUP)]
        for u in range(Q_BLOCK // GRID_W):
            r = (Q_BLOCK // GRID_W) * i + u
            rs = jnp.clip(r - NA_ROWS // 2, 0, rows - NA_ROWS)
            tiles = [[] for _ in range(GQA_GROUP)]
            for jp in range(NA_WIN_ROWS // 2):
                kr = ws + 2 * jp
                ok0 = (kr >= rs) & (kr < rs + NA_ROWS)
                ok1 = (kr + 1 >= rs) & (kr + 1 < rs + NA_ROWS)
                pen = jnp.where(lane < GRID_W, jnp.where(ok0, 0.0, NEG_INF), jnp.where(ok1, 0.0, NEG_INF))
                e = kr - r + (NA_ROWS - 1) - NA_PAIR_LO
                for g in range(GQA_GROUP):
                    tiles[g].append(t_ref[g, e] + pen)
            for g in range(GQA_GROUP):
                blocks[g].append(jnp.concatenate(tiles[g], axis=1))
        bias = jnp.concatenate([b for g in range(GQA_GROUP) for b in blocks[g]], axis=0)
        s = _qk(_stack_heads(q_ref, sub * Q_BLOCK, Q_BLOCK), kw) + bias
        m = jnp.max(s, axis=-1, keepdims=True)
        p = jnp.exp2(s - m)
        l = jnp.sum(p, axis=-1, keepdims=True)
        o = jnp.dot(p.astype(BF16), vw, preferred_element_type=F32) / l
        _unstack_heads(o, o_ref, sub * Q_BLOCK, Q_BLOCK)


def _attn_nbr(qkv, rpb, seq):
    n = qkv.shape[0]
    batch = n // seq
    n_kv = D_MODEL // (GQA_GROUP * HEAD_DIM)
    k_off = D_MODEL // HEAD_DIM
    v_off = k_off + n_kv
    tq = min(NA_BLOCKS_PER_STEP * Q_BLOCK, seq)
    nq = seq // tq
    gw = GQA_GROUP * HEAD_DIM
    rows = seq // GRID_W
    assert rows >= NA_WIN_ROWS and Q_BLOCK == 2 * GRID_W and rpb.shape[1:] == (NA_DR, NA_DC)
    table = _na_table(rpb)
    return pl.pallas_call(
        functools.partial(_attn_nbr_body, rows=rows),
        grid=(n_kv, batch, nq),
        in_specs=[pl.BlockSpec((tq, gw), lambda h, b, i: (b * nq + i, h)),
                  pl.BlockSpec((seq, HEAD_DIM), lambda h, b, i: (b, k_off + h)),
                  pl.BlockSpec((seq, HEAD_DIM), lambda h, b, i: (b, v_off + h)),
                  pl.BlockSpec((GQA_GROUP, NA_PAIRS, GRID_W, 2 * GRID_W), lambda h, b, i: (h, 0, 0, 0))],
        out_specs=pl.BlockSpec((tq, gw), lambda h, b, i: (b * nq + i, h)),
        out_shape=jax.ShapeDtypeStruct((n, D_MODEL), BF16),
        compiler_params=_params("parallel", "parallel", "parallel"),
        name="attn_nbr",
    )(qkv, qkv, qkv, table)


def _sgu_body(u_ref, v_ref, vg_ref, ws_ref, bs_ref, o_ref):
    v = v_ref[...]
    ms = jnp.mean(v * v, axis=-1, keepdims=True)
    vn = (v * lax.rsqrt(ms + RMS_EPS) * vg_ref[...]).astype(BF16)
    cg = v.shape[1] // SG_GROUPS
    for g in range(SG_GROUPS):
        mixed = jnp.dot(ws_ref[g], vn[:, g * cg:(g + 1) * cg], preferred_element_type=F32)
        mixed = mixed + bs_ref[:, g:g + 1]
        o_ref[:, g * cg:(g + 1) * cg] = (u_ref[:, g * cg:(g + 1) * cg] * mixed).astype(o_ref.dtype)


def _sgu(z, v_gain, w_s, b_s):
    n = z.shape[0]
    width = z.shape[1] // 2
    return pl.pallas_call(
        _sgu_body,
        grid=(n // SG_CHUNK,),
        in_specs=[pl.BlockSpec((SG_CHUNK, width), lambda c: (c, 0)),
                  pl.BlockSpec((SG_CHUNK, width), lambda c: (c, 1)),
                  pl.BlockSpec((1, width), lambda c: (0, 0)),
                  pl.BlockSpec((SG_GROUPS, SG_CHUNK, SG_CHUNK), lambda c: (0, 0, 0)),
                  pl.BlockSpec((SG_CHUNK, SG_GROUPS), lambda c: (0, 0))],
        out_specs=pl.BlockSpec((SG_CHUNK, width), lambda c: (c, 0)),
        out_shape=jax.ShapeDtypeStruct((n, width), BF16),
        compiler_params=_params("parallel"),
        name="sgu",
    )(z, z, v_gain.reshape(1, width), w_s.astype(BF16), b_s.T)


def _pack_bf16_pair(lo, hi):
    lo_bits = lax.bitcast_convert_type(lo.astype(BF16).astype(F32), PACKED)
    hi_bits = lax.bitcast_convert_type(hi.astype(BF16).astype(F32), PACKED)
    return (hi_bits & jnp.uint32(0xFFFF0000)) | (lo_bits >> 16)


def _unpack_bf16_pair_f32(words):
    lo = lax.bitcast_convert_type(words << 16, F32)
    hi = lax.bitcast_convert_type(words & jnp.uint32(0xFFFF0000), F32)
    return lo, hi


def _unpack_bf16_pair(words):
    lo, hi = _unpack_bf16_pair_f32(words)
    return lo.astype(BF16), hi.astype(BF16)


def _router_body(x_ref, g_ref, wr_ref, br_ref, xn_ref, ids_ref, gates_ref, counts_ref, carry):
    @pl.when(pl.program_id(0) == 0)
    def _():
        carry[...] = jnp.zeros(carry.shape, carry.dtype)

    x = x_ref[...]
    ms = jnp.mean(x * x, axis=-1, keepdims=True)
    xn = x * lax.rsqrt(ms + RMS_EPS) * g_ref[...]
    half_d = xn.shape[1] // 2
    xn_ref[...] = _pack_bf16_pair(xn[:, :half_d], xn[:, half_d:])
    xh = xn.astype(BF16)
    xl = (xn - xh.astype(F32)).astype(BF16)
    w = wr_ref[...]
    wh = w.astype(BF16)
    wl = (w - wh.astype(F32)).astype(BF16)
    hi = jnp.dot(xh, jnp.concatenate([wh, wl], axis=1), preferred_element_type=F32)
    lo = jnp.dot(xl, wh, preferred_element_type=F32)
    logits = (hi[:, :LANES] + (hi[:, LANES:] + lo)) + br_ref[...]
    lane = lax.broadcasted_iota(jnp.int32, logits.shape, 1)
    ninf = jnp.float32(-jnp.inf)

    lg = jnp.where(lane < N_GROUPS, logits, ninf)
    mg = jnp.max(lg, axis=-1, keepdims=True)
    grp = jnp.min(jnp.where(lg == mg, lane, LANES), axis=-1, keepdims=True)
    p_grp = 1.0 / jnp.sum(jnp.exp(lg - mg), axis=-1, keepdims=True)

    lo = N_GROUPS + grp * EXPERTS_PER_GROUP
    le = jnp.where((lane >= lo) & (lane < lo + EXPERTS_PER_GROUP), logits, ninf)
    m1 = jnp.max(le, axis=-1, keepdims=True)
    i1 = jnp.min(jnp.where(le == m1, lane, LANES), axis=-1, keepdims=True)
    le2 = jnp.where(lane == i1, ninf, le)
    m2 = jnp.max(le2, axis=-1, keepdims=True)
    i2 = jnp.min(jnp.where(le2 == m2, lane, LANES), axis=-1, keepdims=True)
    z = jnp.sum(jnp.exp(le - m1), axis=-1, keepdims=True)
    p1 = 1.0 / z
    p2 = jnp.exp(m2 - m1) / z
    tot = p1 + p2
    g1 = p_grp * (p1 / tot)
    g2 = p_grp * (p2 / tot)
    gates_ref[...] = jnp.where(lane == 0, g1, jnp.where(lane == 1, g2, 0.0))

    e0 = i1 - N_GROUPS
    e1 = i2 - N_GROUPS
    tm = x.shape[0]
    onehot = jnp.where((lane == e0) | (lane == e1), 1.0, 0.0)
    tri = jnp.where(lax.broadcasted_iota(jnp.int32, (tm, tm), 0) > lax.broadcasted_iota(jnp.int32, (tm, tm), 1),
                    1.0, 0.0).astype(BF16)
    before = jnp.dot(tri, onehot.astype(BF16), preferred_element_type=F32) + carry[...]
    r0 = jnp.sum(jnp.where(lane == e0, before, 0.0), axis=-1, keepdims=True).astype(jnp.int32)
    r1 = jnp.sum(jnp.where(lane == e1, before, 0.0), axis=-1, keepdims=True).astype(jnp.int32)
    carry[...] = carry[...] + jnp.sum(onehot, axis=0, keepdims=True)
    counts_ref[...] = carry[...].astype(jnp.int32)
    ids_ref[...] = jnp.where(lane == 0, e0, jnp.where(lane == 1, e1, jnp.where(lane == 2, r0, jnp.where(lane == 3, r1, 0))))


def _router(h, gain, w_group, b_group, w_expert, b_expert):
    n, d = h.shape
    tm = min(256, n)
    wr = jnp.concatenate([w_group, w_expert.transpose(1, 0, 2).reshape(d, N_EXPERTS)], axis=1)
    wr = jnp.pad(wr.astype(F32), ((0, 0), (0, LANES - wr.shape[1])))
    br = jnp.concatenate([b_group, b_expert.reshape(N_EXPERTS)]).astype(F32)
    br = jnp.pad(br, (0, LANES - br.shape[0])).reshape(1, LANES)
    return pl.pallas_call(
        _router_body,
        grid=(n // tm,),
        in_specs=[pl.BlockSpec((tm, d), lambda i: (i, 0)),
                  pl.BlockSpec((1, d), lambda i: (0, 0)),
                  pl.BlockSpec((d, LANES), lambda i: (0, 0)),
                  pl.BlockSpec((1, LANES), lambda i: (0, 0))],
        out_specs=[pl.BlockSpec((tm, d // 2), lambda i: (i, 0)),
                   pl.BlockSpec((tm, LANES), lambda i: (i, 0)),
                   pl.BlockSpec((tm, LANES), lambda i: (i, 0)),
                   pl.BlockSpec((1, LANES), lambda i: (0, 0))],
        out_shape=[jax.ShapeDtypeStruct((n, d // 2), PACKED),
                   jax.ShapeDtypeStruct((n, LANES), jnp.int32),
                   jax.ShapeDtypeStruct((n, LANES), F32),
                   jax.ShapeDtypeStruct((1, LANES), jnp.int32)],
        scratch_shapes=[pltpu.VMEM((1, LANES), F32)],
        compiler_params=_params("arbitrary"),
        name="norm_router",
    )(h, gain.reshape(1, d), wr, br)


def _dispatch_plan(ids, counts):
    n = ids.shape[0]
    nka = n * 2
    nb = (nka + MOE_ROWS - 1) // MOE_ROWS + N_EXPERTS + GATHER_AHEAD
    e, rank = ids[:, :2], ids[:, 2:4]
    blocks = (counts + MOE_ROWS - 1) // MOE_ROWS
    blk_end = jnp.cumsum(blocks)
    blk_start = blk_end - blocks
    n_used = blk_end[-1]
    lanes = jnp.arange(N_EXPERTS, dtype=jnp.int32)
    row_start = jnp.sum(jnp.where(e[:, :, None] == lanes, blk_start * MOE_ROWS, 0), axis=-1)
    dest = (row_start + rank).astype(jnp.int32).reshape(nka)
    flat_tok = jnp.arange(nka, dtype=jnp.int32) // 2
    row_tok = jnp.zeros((nb * MOE_ROWS,), jnp.int32).at[dest].set(flat_tok)
    b = jnp.arange(nb, dtype=jnp.int32)
    block_e = jnp.minimum(jnp.sum(b[:, None] >= blk_end[None, :], axis=1), N_EXPERTS - 1).astype(jnp.int32)
    first = ((b == blk_start[block_e]) & (b < n_used)).astype(jnp.int32)
    nxt_blk = blk_end[block_e]
    nxt_e = jnp.where(nxt_blk < n_used, block_e[jnp.minimum(nxt_blk, nb - 1)], -1).astype(jnp.int32)
    return dest, row_tok, block_e, first, nxt_e, n_used.astype(jnp.int32).reshape(1), nb


def _row_copy(src_hbm, row, dst, dst_row, sem):
    return pltpu.make_async_copy(src_hbm.at[pl.ds(row, 1), :], dst.at[pl.ds(dst_row, 1), :], sem)


def _experts_body(be_ref, first_ref, nxt_ref, tok_ref, nu_ref, x_hbm, w1_hbm, w3_hbm, w2_hbm, o_ref,
                  xbuf, st13, st2, wb13, wb2, gsem, wsem, *, layer):
    i = pl.program_id(0)
    n_used = nu_ref[0]
    rows = xbuf.shape[1]
    de = st2.shape[0]
    n_slots = xbuf.shape[0]
    slot = lax.rem(i, n_slots)
    ahead_slot = lax.rem(i + GATHER_AHEAD, n_slots)

    def weight_copies(e):
        return (pltpu.make_async_copy(w1_hbm.at[layer, e], st13.at[0], wsem.at[0]),
                pltpu.make_async_copy(w3_hbm.at[layer, e], st13.at[1], wsem.at[1]),
                pltpu.make_async_copy(w2_hbm.at[layer, e], st2, wsem.at[2]))

    def start_gather(blk, dst_slot, lo=0, hi=rows):
        for r in range(lo, hi):
            _row_copy(x_hbm, tok_ref[blk * rows + r], xbuf.at[dst_slot], r, gsem.at[dst_slot]).start()

    @pl.when(i == 0)
    def _():
        for c in weight_copies(be_ref[0]):
            c.start(priority=WEIGHT_DMA_PRIORITY)
        for b in range(GATHER_AHEAD):
            start_gather(b, b)

    @pl.when(i < n_used + GATHER_AHEAD)
    def _():
        for r in range(rows):
            _row_copy(x_hbm, 0, xbuf.at[slot], r, gsem.at[slot]).wait()

    @pl.when(i < n_used)
    def _():
        @pl.when(first_ref[i] == 1)
        def _():
            for c in weight_copies(0):
                c.wait()
            wb13[:, :de] = st13[0].astype(BF16)
            wb13[:, de:] = st13[1].astype(BF16)
            wb2[...] = st2[...].astype(BF16)

            @pl.when(nxt_ref[i] >= 0)
            def _():
                for c in weight_copies(nxt_ref[i]):
                    c.start(priority=WEIGHT_DMA_PRIORITY)

        pieces = 8
        per = rows // pieces
        x = jnp.concatenate(_unpack_bf16_pair(xbuf[slot]), axis=1)
        half = rows // 2
        wide = (2 * de) // (2 * MXU_TILE) * (2 * MXU_TILE)
        start_gather(i + GATHER_AHEAD, ahead_slot, 0, per)
        h_a = jnp.dot(x, wb13[:, :wide], preferred_element_type=F32)
        start_gather(i + GATHER_AHEAD, ahead_slot, per, 2 * per)
        h_b = jnp.concatenate(
            [jnp.dot(x[:half], wb13[:, wide:], preferred_element_type=F32),
             jnp.dot(x[half:], wb13[:, wide:], preferred_element_type=F32)], axis=0)
        h13 = jnp.concatenate([h_a, h_b], axis=1)
        start_gather(i + GATHER_AHEAD, ahead_slot, 2 * per, 4 * per)
        h1, h3 = h13[:, :de], h13[:, de:]
        a = ((h1 / (1.0 + jnp.exp(-h1))) * h3).astype(BF16)
        half_d = wb2.shape[1] // 2
        cols = half_d // 2
        for c in range(2):
            lo = jnp.dot(a, wb2[:, c * cols:(c + 1) * cols], preferred_element_type=F32)
            start_gather(i + GATHER_AHEAD, ahead_slot, (4 + 2 * c) * per, (5 + 2 * c) * per)
            hi = jnp.dot(a, wb2[:, half_d + c * cols:half_d + (c + 1) * cols], preferred_element_type=F32)
            start_gather(i + GATHER_AHEAD, ahead_slot, (5 + 2 * c) * per, (6 + 2 * c) * per)
            o_ref[:, c * cols:(c + 1) * cols] = _pack_bf16_pair(lo, hi)

    @pl.when(i >= n_used)
    def _():
        o_ref[...] = jnp.zeros(o_ref.shape, o_ref.dtype)


def _experts(xn, row_tok, block_e, first, nxt_e, n_used, nb, w1, w3, w2, *, layer):
    d, de = w1.shape[2:]
    assert xn.shape[1] * 2 == d
    grid_spec = pltpu.PrefetchScalarGridSpec(
        num_scalar_prefetch=5,
        grid=(nb,),
        in_specs=[pl.BlockSpec(memory_space=pl.ANY)] * 4,
        out_specs=pl.BlockSpec((MOE_ROWS, d // 2), lambda i, *_: (i, 0)),
        scratch_shapes=[pltpu.VMEM((GATHER_AHEAD + 1, MOE_ROWS, d // 2), PACKED),
                        pltpu.VMEM((2, d, de), F32),
                        pltpu.VMEM((de, d), F32),
                        pltpu.VMEM((d, 2 * de), BF16),
                        pltpu.VMEM((de, d), BF16),
                        pltpu.SemaphoreType.DMA((GATHER_AHEAD + 1,)),
                        pltpu.SemaphoreType.DMA((3,))],
    )
    return pl.pallas_call(
        functools.partial(_experts_body, layer=layer),
        grid_spec=grid_spec,
        out_shape=jax.ShapeDtypeStruct((nb * MOE_ROWS, d // 2), PACKED),
        compiler_params=pltpu.CompilerParams(dimension_semantics=("arbitrary",),
                                             vmem_limit_bytes=EXPERTS_VMEM_LIMIT),
        name="experts",
    )(block_e, first, nxt_e, row_tok, n_used, xn, w1, w3, w2)


def _combine_body(pos_ref, h_ref, g_ref, gain_ref, y_hbm, *rest, last):
    if last:
        n_ref, ybuf, sem = rest
    else:
        o_ref, n_ref, ybuf, sem = rest
    i = pl.program_id(0)
    n_steps = pl.num_programs(0)
    tm = h_ref.shape[0]

    def start_gather(step, slot):
        for r in range(tm):
            base = (step * tm + r) * 2
            _row_copy(y_hbm, pos_ref[base], ybuf.at[slot, 0], r, sem.at[slot]).start()
            _row_copy(y_hbm, pos_ref[base + 1], ybuf.at[slot, 1], r, sem.at[slot]).start()

    def wait_gather(slot):
        for r in range(tm):
            _row_copy(y_hbm, 0, ybuf.at[slot, 0], r, sem.at[slot]).wait()
            _row_copy(y_hbm, 0, ybuf.at[slot, 1], r, sem.at[slot]).wait()

    slot = i % 2

    @pl.when(i == 0)
    def _():
        start_gather(0, 0)

    @pl.when(i + 1 < n_steps)
    def _():
        start_gather(i + 1, 1 - slot)

    wait_gather(slot)
    g = g_ref[...]
    lo0, hi0 = _unpack_bf16_pair_f32(ybuf[slot, 0])
    lo1, hi1 = _unpack_bf16_pair_f32(ybuf[slot, 1])
    moe = jnp.concatenate([g[:, 0:1] * lo0 + g[:, 1:2] * lo1, g[:, 0:1] * hi0 + g[:, 1:2] * hi1], axis=1)
    out = h_ref[...] + moe
    if not last:
        o_ref[...] = out
    ms = jnp.mean(out * out, axis=-1, keepdims=True)
    n_ref[...] = (out * lax.rsqrt(ms + RMS_EPS) * gain_ref[...]).astype(n_ref.dtype)


def _combine(h, gates, pos, yb, next_gain, *, last):
    n, d = h.shape
    tm = min(128, n)
    row_spec = pl.BlockSpec((tm, d), lambda i, pos_: (i, 0))
    grid_spec = pltpu.PrefetchScalarGridSpec(
        num_scalar_prefetch=1,
        grid=(n // tm,),
        in_specs=[row_spec,
                  pl.BlockSpec((tm, LANES), lambda i, pos_: (i, 0)),
                  pl.BlockSpec((1, d), lambda i, pos_: (0, 0)),
                  pl.BlockSpec(memory_space=pl.ANY)],
        out_specs=[row_spec] if last else [row_spec, row_spec],
        scratch_shapes=[pltpu.VMEM((2, 2, tm, d // 2), PACKED),
                        pltpu.SemaphoreType.DMA((2,))],
    )
    normed = jax.ShapeDtypeStruct((n, d), F32 if last else BF16)
    return pl.pallas_call(
        functools.partial(_combine_body, last=last),
        grid_spec=grid_spec,
        out_shape=[normed] if last else [jax.ShapeDtypeStruct((n, d), F32), normed],
        compiler_params=_params("arbitrary"),
        name="moe_combine",
    )(pos, h, gates, next_gain.reshape(1, d), yb)


def _hierarchical_moe(h, gain, w_group, b_group, w_expert, b_expert, w1, w3, w2, next_gain, *, layer, last):
    xn, ids, gates, counts = _router(h, gain, w_group, b_group, w_expert, b_expert)
    dest, row_tok, block_e, first, nxt_e, n_used, nb = _dispatch_plan(ids[:, :4], counts[0, :N_EXPERTS])
    yb = _experts(xn, row_tok, block_e, first, nxt_e, n_used, nb, w1, w3, w2, layer=layer)
    return _combine(h, gates, dest, yb, next_gain, last=last)


def kernel(x, norm_mix, norm_ffn, norm_final, a_w_in, a_q_gain, a_k_gain, a_w_o, b_w_in, b_v_gain, b_w_s, b_bias, b_w_o, c_w_in, c_sink, c_w_o, d_w_in, d_rpb, d_w_o, moe_w_group, moe_b_group, moe_w_expert, moe_b_expert, moe_w1, moe_w3, moe_w2):
    batch, seq, d = x.shape
    h = x.reshape(batch * seq, d)
    ones = jnp.ones((HEAD_DIM,), F32)
    depth = norm_mix.shape[0]
    hn = _rmsnorm(h, norm_mix[0], BF16)
    for i in range(depth):
        m, j = i % N_MIXERS, i // N_MIXERS
        if m == 0:
            cos, sin = _axial_tables(seq)
            qkv = _matmul_qkv(hn, a_w_in[j], cos, sin, a_q_gain[j], a_k_gain[j],
                              mode="axial", seq=seq)
            mixed, w_o = _attn_dense(qkv, seq), a_w_o[j]
        elif m == 1:
            z = _matmul_gelu(hn, b_w_in[j], F32)
            mixed, w_o = _sgu(z, b_v_gain[j], b_w_s[j], b_bias[j]), b_w_o[j]
        elif m == 2:
            cos, sin = _rope_tables(seq)
            qkv = _matmul_qkv(hn, c_w_in[j], cos, sin, ones, ones, mode="rope", seq=seq)
            mixed, w_o = _attn_window(qkv, c_sink[j].astype(F32), seq), c_w_o[j]
        else:
            cos, sin = _rope_tables(seq)
            qkv = _matmul_qkv(hn, d_w_in[j], cos, sin, ones, ones, mode="none", seq=seq)
            mixed, w_o = _attn_nbr(qkv, d_rpb[j], seq), d_w_o[j]
        h = _matmul_residual(mixed, w_o, h)
        last = i == depth - 1
        outs = _hierarchical_moe(h, norm_ffn[i], moe_w_group[i], moe_b_group[i], moe_w_expert[i],
                                 moe_b_expert[i], moe_w1, moe_w3, moe_w2,
                                 norm_final if last else norm_mix[i + 1], layer=i, last=last)
        if last:
            return outs[0].reshape(batch, seq, d)
        h, hn = outs
```

```python
import functools

import jax
import jax.numpy as jnp
from jax import lax
from jax.experimental import pallas as pl
from jax.experimental.pallas import tpu as pltpu

D_MODEL = 4096
BATCH = 2
SEQ = 4096
DEPTH = 4
N_MIXERS = 4
HEAD_DIM = 128
GQA_GROUP = 4
ROPE_THETA = 10000.0
Q_BLOCK = 128
GRID_W = 64
SG_CHUNK = 128
SG_GROUPS = 8
WINDOW = 128
NA_ROWS = 8
NA_COLS = 16
N_GROUPS = 8
EXPERTS_PER_GROUP = 4
N_EXPERTS = N_GROUPS * EXPERTS_PER_GROUP
D_EXPERT = 384
MOE_ROWS = 256
RMS_EPS = 1e-6
NEG_INF = -1e30

LOG2_E = 1.4426950408889634
LANES = 128
MXU_TILE = 256
DENSE_KEY_CHUNK = 1024
DENSE_Q_ROWS = 512
NA_WIN_ROWS = 10
WINDOW_Q_ROWS = 256
WINDOW_BLOCKS_PER_STEP = 8
NA_BLOCKS_PER_STEP = 8
VMEM_LIMIT = 48 * 1024 * 1024
EXPERTS_VMEM_LIMIT = 56 * 1024 * 1024

F32 = jnp.float32
BF16 = jnp.bfloat16
PACKED = jnp.uint32


def _params(*sem):
    return pltpu.CompilerParams(dimension_semantics=sem, vmem_limit_bytes=VMEM_LIMIT)


def _rmsnorm_body(x_ref, g_ref, o_ref):
    x = x_ref[...]
    ms = jnp.mean(x * x, axis=-1, keepdims=True)
    o_ref[...] = (x * lax.rsqrt(ms + RMS_EPS) * g_ref[...]).astype(o_ref.dtype)


def _rmsnorm(x, gain, out_dtype):
    n, d = x.shape
    tm = min(256, n)
    return pl.pallas_call(
        _rmsnorm_body,
        grid=(n // tm,),
        in_specs=[pl.BlockSpec((tm, d), lambda i: (i, 0)),
                  pl.BlockSpec((1, d), lambda i: (0, 0))],
        out_specs=pl.BlockSpec((tm, d), lambda i: (i, 0)),
        out_shape=jax.ShapeDtypeStruct((n, d), out_dtype),
        compiler_params=_params("parallel"),
        name="rmsnorm",
    )(x, gain.reshape(1, d))


def _gelu_tanh(x):
    return 0.5 * x * (1.0 + jnp.tanh(0.7978845608028654 * (x + 0.044715 * (x * x * x))))


def _mm_gelu_body(x_ref, w_ref, o_ref):
    acc = jnp.dot(x_ref[...], w_ref[...].astype(BF16), preferred_element_type=F32)
    o_ref[...] = _gelu_tanh(acc).astype(o_ref.dtype)


def _mm_res_body(x_ref, w_ref, r_ref, o_ref):
    acc = jnp.dot(x_ref[...], w_ref[...].astype(BF16), preferred_element_type=F32)
    o_ref[...] = r_ref[...] + acc


def _swap_halves(y, width):
    if 2 * width == LANES:
        return pltpu.roll(y, width, axis=1)
    lane = lax.broadcasted_iota(jnp.int32, y.shape, 1)
    first = (lane % (2 * width)) < width
    return jnp.where(first, pltpu.roll(y, LANES - width, axis=1), pltpu.roll(y, width, axis=1))


def _mm_qkv_body(x_ref, w_ref, cos_ref, sin_ref, gq_ref, gk_ref, o_ref, *, mode, n_q, n_qk, scale):
    j = pl.program_id(1)
    is_q = j < n_q
    is_qk = j < n_qk
    x = x_ref[...]
    tn = o_ref.shape[1]

    def head_post(a):
        if mode == "axial":
            gain = jnp.where(is_q, gq_ref[...] * scale, gk_ref[...])
            a = a * lax.rsqrt(jnp.mean(a * a, axis=-1, keepdims=True) + RMS_EPS)
        else:
            gain = jnp.where(is_q, jnp.full((1, HEAD_DIM), scale, F32), jnp.ones((1, HEAD_DIM), F32))
        a = a * gain
        if mode == "axial":
            a = a * cos_ref[...] + _swap_halves(a, HEAD_DIM // 4) * sin_ref[...]
        elif mode == "rope":
            a = a * cos_ref[...] + _swap_halves(a, HEAD_DIM // 2) * sin_ref[...]
        return a

    if mode == "axial":
        acc = jnp.dot(x, w_ref[...].astype(BF16), preferred_element_type=F32)

        @pl.when(is_qk)
        def _():
            for t in range(tn // HEAD_DIM):
                o_ref[:, t * HEAD_DIM:(t + 1) * HEAD_DIM] = head_post(
                    acc[:, t * HEAD_DIM:(t + 1) * HEAD_DIM]).astype(o_ref.dtype)

        @pl.when(jnp.logical_not(is_qk))
        def _():
            o_ref[...] = acc.astype(o_ref.dtype)
    else:
        piece = min(MXU_TILE, tn)
        for pc in range(tn // piece):
            acc = jnp.dot(x, w_ref[:, pc * piece:(pc + 1) * piece].astype(BF16), preferred_element_type=F32)
            for t in range(piece // HEAD_DIM):
                raw = acc[:, t * HEAD_DIM:(t + 1) * HEAD_DIM]
                lo = pc * piece + t * HEAD_DIM
                o_ref[:, lo:lo + HEAD_DIM] = jnp.where(is_qk, head_post(raw), raw).astype(o_ref.dtype)


def _mm_tiles(m, n):
    return min(1024, m), min(512, n)


def _matmul_gelu(x, w, out_dtype):
    m, k = x.shape
    n = w.shape[1]
    tm, tn = _mm_tiles(m, n)
    return pl.pallas_call(
        _mm_gelu_body,
        grid=(m // tm, n // tn),
        in_specs=[pl.BlockSpec((tm, k), lambda i, j: (i, 0)),
                  pl.BlockSpec((k, tn), lambda i, j: (0, j))],
        out_specs=pl.BlockSpec((tm, tn), lambda i, j: (i, j)),
        out_shape=jax.ShapeDtypeStruct((m, n), out_dtype),
        compiler_params=_params("parallel", "parallel"),
        name="matmul_gelu",
    )(x, w)


def _matmul_residual(x, w, res):
    m, k = x.shape
    n = w.shape[1]
    tm, tn = _mm_tiles(m, n)
    return pl.pallas_call(
        _mm_res_body,
        grid=(m // tm, n // tn),
        in_specs=[pl.BlockSpec((tm, k), lambda i, j: (i, 0)),
                  pl.BlockSpec((k, tn), lambda i, j: (0, j)),
                  pl.BlockSpec((tm, tn), lambda i, j: (i, j))],
        out_specs=pl.BlockSpec((tm, tn), lambda i, j: (i, j)),
        out_shape=jax.ShapeDtypeStruct((m, n), F32),
        compiler_params=_params("parallel", "parallel"),
        name="matmul_residual",
    )(x, w, res)


def _matmul_qkv(x, w, cos, sin, gq, gk, *, mode, seq):
    m, k = x.shape
    n = w.shape[1]
    d_q = D_MODEL
    d_kv = (n - d_q) // 2
    tm = min(1024, seq)
    tn = min(512, d_kv)
    s_blocks = seq // tm
    scale = HEAD_DIM ** -0.5 * LOG2_E
    body = functools.partial(_mm_qkv_body, mode=mode, n_q=d_q // tn, n_qk=(d_q + d_kv) // tn, scale=scale)
    return pl.pallas_call(
        body,
        grid=(m // tm, n // tn),
        in_specs=[pl.BlockSpec((tm, k), lambda i, j: (i, 0)),
                  pl.BlockSpec((k, tn), lambda i, j: (0, j)),
                  pl.BlockSpec((tm, HEAD_DIM), lambda i, j: (i % s_blocks, 0)),
                  pl.BlockSpec((tm, HEAD_DIM), lambda i, j: (i % s_blocks, 0)),
                  pl.BlockSpec((1, HEAD_DIM), lambda i, j: (0, 0)),
                  pl.BlockSpec((1, HEAD_DIM), lambda i, j: (0, 0))],
        out_specs=pl.BlockSpec((tm, tn), lambda i, j: (i, j)),
        out_shape=jax.ShapeDtypeStruct((m, n), BF16),
        compiler_params=_params("parallel", "parallel"),
        name="matmul_qkv_" + mode,
    )(x, w, cos, sin, gq.reshape(1, HEAD_DIM), gk.reshape(1, HEAD_DIM))


def _rope_cos_sin(pos, dim):
    half = dim // 2
    inv = jnp.power(jnp.float32(ROPE_THETA), -jnp.arange(half, dtype=F32) * (2.0 / dim))
    ang = pos.astype(F32)[:, None] * inv[None, :]
    return jnp.cos(ang), jnp.sin(ang)


def _axial_tables(seq):
    t = jnp.arange(seq, dtype=jnp.int32)
    cr, sr = _rope_cos_sin(t // GRID_W, HEAD_DIM // 2)
    cc, sc = _rope_cos_sin(t % GRID_W, HEAD_DIM // 2)
    return (jnp.concatenate([cr, cr, cc, cc], axis=-1),
            jnp.concatenate([-sr, sr, -sc, sc], axis=-1))


def _rope_tables(seq):
    c, s = _rope_cos_sin(jnp.arange(seq, dtype=jnp.int32), HEAD_DIM)
    return jnp.concatenate([c, c], axis=-1), jnp.concatenate([-s, s], axis=-1)


def _stack_heads(q_ref, row0=0, rows=None):
    rows = q_ref.shape[0] if rows is None else rows
    return jnp.concatenate(
        [q_ref[row0:row0 + rows, g * HEAD_DIM:(g + 1) * HEAD_DIM] for g in range(GQA_GROUP)], axis=0)


def _unstack_heads(o, o_ref, row0=0, rows=None):
    rows = o_ref.shape[0] if rows is None else rows
    for g in range(GQA_GROUP):
        o_ref[row0:row0 + rows, g * HEAD_DIM:(g + 1) * HEAD_DIM] = o[g * rows:(g + 1) * rows].astype(o_ref.dtype)


def _qk(q, k):
    return lax.dot_general(q, k, (((1,), (1,)), ((), ())), preferred_element_type=F32)


def _attn_dense_body(q_ref, k_ref, v_ref, o_ref):
    q = _stack_heads(q_ref)
    seq = k_ref.shape[0]
    kc = min(DENSE_KEY_CHUNK, seq)
    m = l = acc = None
    for c in range(seq // kc):
        s = _qk(q, k_ref[c * kc:(c + 1) * kc, :])
        mc = jnp.max(s, axis=-1, keepdims=True)
        if c == 0:
            m = mc
            p = jnp.exp2(s - m)
            l = jnp.sum(p, axis=-1, keepdims=True)
            acc = jnp.dot(p.astype(BF16), v_ref[c * kc:(c + 1) * kc, :], preferred_element_type=F32)
        else:
            m_new = jnp.maximum(m, mc)
            alpha = jnp.exp2(m - m_new)
            p = jnp.exp2(s - m_new)
            l = alpha * l + jnp.sum(p, axis=-1, keepdims=True)
            acc = alpha * acc + jnp.dot(p.astype(BF16), v_ref[c * kc:(c + 1) * kc, :],
                                        preferred_element_type=F32)
            m = m_new
    _unstack_heads(acc / l, o_ref)


def _attn_dense(qkv, seq):
    n = qkv.shape[0]
    batch = n // seq
    n_kv = D_MODEL // (GQA_GROUP * HEAD_DIM)
    k_off = D_MODEL // HEAD_DIM
    v_off = k_off + n_kv
    tq = DENSE_Q_ROWS
    nq = seq // tq
    gw = GQA_GROUP * HEAD_DIM
    return pl.pallas_call(
        _attn_dense_body,
        grid=(batch, n_kv, nq),
        in_specs=[pl.BlockSpec((tq, gw), lambda b, h, i: (b * nq + i, h)),
                  pl.BlockSpec((seq, HEAD_DIM), lambda b, h, i: (b, k_off + h)),
                  pl.BlockSpec((seq, HEAD_DIM), lambda b, h, i: (b, v_off + h))],
        out_specs=pl.BlockSpec((tq, gw), lambda b, h, i: (b * nq + i, h)),
        out_shape=jax.ShapeDtypeStruct((n, D_MODEL), BF16),
        compiler_params=_params("parallel", "parallel", "parallel"),
        name="attn_dense",
    )(qkv, qkv, qkv)


def _attn_window_body(sink_ref, q_ref, k_ref, v_ref, o_ref):
    h = pl.program_id(1)
    tq = WINDOW_Q_ROWS
    nkeys = tq + 2 * WINDOW
    shape = (GQA_GROUP * tq, nkeys)
    rel = (lax.broadcasted_iota(jnp.int32, shape, 0) % tq
           - lax.broadcasted_iota(jnp.int32, shape, 1) + WINDOW)
    rowg = lax.broadcasted_iota(jnp.int32, (shape[0], 1), 0) // tq
    sink = jnp.zeros((shape[0], 1), F32)
    for g in range(GQA_GROUP):
        sink = jnp.where(rowg == g, sink_ref[h * GQA_GROUP + g] * LOG2_E, sink)
    for sub in range(q_ref.shape[0] // tq):
        i = pl.program_id(2) * (q_ref.shape[0] // tq) + sub
        start = pl.multiple_of(jnp.clip(i * tq - WINDOW, 0, k_ref.shape[0] - nkeys), WINDOW)
        k3 = k_ref[pl.ds(start, nkeys), :]
        v3 = v_ref[pl.ds(start, nkeys), :]
        s = _qk(_stack_heads(q_ref, sub * tq, tq), k3)
        in_window = (rel + (i * tq - start)).astype(jnp.uint32) <= 2 * WINDOW
        s = jnp.where(in_window, s, NEG_INF)
        m = jnp.maximum(jnp.max(s, axis=-1, keepdims=True), sink)
        p = jnp.exp2(s - m)
        denom = jnp.sum(p, axis=-1, keepdims=True) + jnp.exp2(sink - m)
        o = jnp.dot(p.astype(BF16), v3, preferred_element_type=F32) / denom
        _unstack_heads(o, o_ref, sub * tq, tq)


def _attn_window(qkv, sink, seq):
    n = qkv.shape[0]
    batch = n // seq
    n_kv = D_MODEL // (GQA_GROUP * HEAD_DIM)
    k_off = D_MODEL // HEAD_DIM
    v_off = k_off + n_kv
    tq = min(WINDOW_BLOCKS_PER_STEP * WINDOW_Q_ROWS, seq)
    nq = seq // tq
    gw = GQA_GROUP * HEAD_DIM
    assert seq >= WINDOW_Q_ROWS + 2 * WINDOW
    grid_spec = pltpu.PrefetchScalarGridSpec(
        num_scalar_prefetch=1,
        grid=(batch, n_kv, nq),
        in_specs=[pl.BlockSpec((tq, gw), lambda b, h, i, s_: (b * nq + i, h)),
                  pl.BlockSpec((seq, HEAD_DIM), lambda b, h, i, s_: (b, k_off + h)),
                  pl.BlockSpec((seq, HEAD_DIM), lambda b, h, i, s_: (b, v_off + h))],
        out_specs=pl.BlockSpec((tq, gw), lambda b, h, i, s_: (b * nq + i, h)),
    )
    return pl.pallas_call(
        _attn_window_body,
        grid_spec=grid_spec,
        out_shape=jax.ShapeDtypeStruct((n, D_MODEL), BF16),
        compiler_params=_params("parallel", "parallel", "parallel"),
        name="attn_window",
    )(sink, qkv, qkv, qkv)


def _na_window_start(i, rows):
    return jnp.clip(2 * i - NA_ROWS // 2, 0, rows - NA_WIN_ROWS)


NA_DR = 2 * NA_ROWS - 1
NA_DC = 2 * NA_COLS - 1
NA_PAIR_LO = -2
NA_PAIRS = NA_DR + 3


def _na_table_body(rpb_ref, o_ref):
    h = pl.program_id(0)
    shape = (GRID_W, 2 * GRID_W)
    c = lax.broadcasted_iota(jnp.int32, shape, 0)
    lane = lax.broadcasted_iota(jnp.int32, shape, 1)
    kc = lane % GRID_W
    d = kc - c + (NA_COLS - 1)
    cs = jnp.clip(c - NA_COLS // 2, 0, GRID_W - NA_COLS)
    col_ok = (kc >= cs) & (kc < cs + NA_COLS)
    neg = jnp.full(shape, NEG_INF, F32)
    rows = []
    for a in range(NA_DR):
        val = jnp.zeros(shape, F32)
        for b in range(NA_DC):
            val = jnp.where(d == b, rpb_ref[(h * NA_DR + a) * NA_DC + b] * LOG2_E, val)
        rows.append(jnp.where(col_ok, val, neg))
    for e in range(NA_PAIRS):
        a0 = e + NA_PAIR_LO
        lo = rows[a0] if 0 <= a0 < NA_DR else neg
        hi = rows[a0 + 1] if 0 <= a0 + 1 < NA_DR else neg
        o_ref[0, e] = jnp.where(lane >= GRID_W, hi, lo)


def _na_table(rpb):
    n_heads = rpb.shape[0]
    grid_spec = pltpu.PrefetchScalarGridSpec(
        num_scalar_prefetch=1,
        grid=(n_heads,),
        in_specs=[],
        out_specs=pl.BlockSpec((1, NA_PAIRS, GRID_W, 2 * GRID_W), lambda h, r: (h, 0, 0, 0)),
    )
    return pl.pallas_call(
        _na_table_body,
        grid_spec=grid_spec,
        out_shape=jax.ShapeDtypeStruct((n_heads, NA_PAIRS, GRID_W, 2 * GRID_W), F32),
        compiler_params=_params("arbitrary"),
        name="na_bias_table",
    )(rpb.astype(F32).reshape(-1))


def _attn_nbr_body(q_ref, k_ref, v_ref, t_ref, o_ref, *, rows):
    n_sub = q_ref.shape[0] // Q_BLOCK
    for sub in range(n_sub):
        i = pl.program_id(2) * n_sub + sub
        ws = _na_window_start(i, rows)
        start = pl.multiple_of(ws * GRID_W, GRID_W)
        nkeys = NA_WIN_ROWS * GRID_W
        kw = k_ref[pl.ds(start, nkeys), :]
        vw = v_ref[pl.ds(start, nkeys), :]
        lane = lax.broadcasted_iota(jnp.int32, (1, 2 * GRID_W), 1)
        blocks = [[] for _ in range(GQA_GROUP)]
        for u in range(Q_BLOCK // GRID_W):
            r = (Q_BLOCK // GRID_W) * i + u
            rs = jnp.clip(r - NA_ROWS // 2, 0, rows - NA_ROWS)
            tiles = [[] for _ in range(GQA_GROUP)]
            for jp in range(NA_WIN_ROWS // 2):
                kr = ws + 2 * jp
                ok0 = (kr >= rs) & (kr < rs + NA_ROWS)
                ok1 = (kr + 1 >= rs) & (kr + 1 < rs + NA_ROWS)
                pen = jnp.where(lane < GRID_W, jnp.where(ok0, 0.0, NEG_INF), jnp.where(ok1, 0.0, NEG_INF))
                e = kr - r + (NA_ROWS - 1) - NA_PAIR_LO
                for g in range(GQA_GROUP):
                    tiles[g].append(t_ref[g, e] + pen)
            for g in range(GQA_GROUP):
                blocks[g].append(jnp.concatenate(tiles[g], axis=1))
        bias = jnp.concatenate([b for g in range(GQA_GROUP) for b in blocks[g]], axis=0)
        s = _qk(_stack_heads(q_ref, sub * Q_BLOCK, Q_BLOCK), kw) + bias
        m = jnp.max(s, axis=-1, keepdims=True)
        p = jnp.exp2(s - m)
        l = jnp.sum(p, axis=-1, keepdims=True)
        o = jnp.dot(p.astype(BF16), vw, preferred_element_type=F32) / l
        _unstack_heads(o, o_ref, sub * Q_BLOCK, Q_BLOCK)


def _attn_nbr(qkv, rpb, seq):
    n = qkv.shape[0]
    batch = n // seq
    n_kv = D_MODEL // (GQA_GROUP * HEAD_DIM)
    k_off = D_MODEL // HEAD_DIM
    v_off = k_off + n_kv
    tq = min(NA_BLOCKS_PER_STEP * Q_BLOCK, seq)
    nq = seq // tq
    gw = GQA_GROUP * HEAD_DIM
    rows = seq // GRID_W
    assert rows >= NA_WIN_ROWS and Q_BLOCK == 2 * GRID_W and rpb.shape[1:] == (NA_DR, NA_DC)
    table = _na_table(rpb)
    return pl.pallas_call(
        functools.partial(_attn_nbr_body, rows=rows),
        grid=(n_kv, batch, nq),
        in_specs=[pl.BlockSpec((tq, gw), lambda h, b, i: (b * nq + i, h)),
                  pl.BlockSpec((seq, HEAD_DIM), lambda h, b, i: (b, k_off + h)),
                  pl.BlockSpec((seq, HEAD_DIM), lambda h, b, i: (b, v_off + h)),
                  pl.BlockSpec((GQA_GROUP, NA_PAIRS, GRID_W, 2 * GRID_W), lambda h, b, i: (h, 0, 0, 0))],
        out_specs=pl.BlockSpec((tq, gw), lambda h, b, i: (b * nq + i, h)),
        out_shape=jax.ShapeDtypeStruct((n, D_MODEL), BF16),
        compiler_params=_params("parallel", "parallel", "parallel"),
        name="attn_nbr",
    )(qkv, qkv, qkv, table)


def _sgu_body(u_ref, v_ref, vg_ref, ws_ref, bs_ref, o_ref):
    v = v_ref[...]
    ms = jnp.mean(v * v, axis=-1, keepdims=True)
    vn = (v * lax.rsqrt(ms + RMS_EPS) * vg_ref[...]).astype(BF16)
    cg = v.shape[1] // SG_GROUPS
    for g in range(SG_GROUPS):
        mixed = jnp.dot(ws_ref[g], vn[:, g * cg:(g + 1) * cg], preferred_element_type=F32)
        mixed = mixed + bs_ref[:, g:g + 1]
        o_ref[:, g * cg:(g + 1) * cg] = (u_ref[:, g * cg:(g + 1) * cg] * mixed).astype(o_ref.dtype)


def _sgu(z, v_gain, w_s, b_s):
    n = z.shape[0]
    width = z.shape[1] // 2
    return pl.pallas_call(
        _sgu_body,
        grid=(n // SG_CHUNK,),
        in_specs=[pl.BlockSpec((SG_CHUNK, width), lambda c: (c, 0)),
                  pl.BlockSpec((SG_CHUNK, width), lambda c: (c, 1)),
                  pl.BlockSpec((1, width), lambda c: (0, 0)),
                  pl.BlockSpec((SG_GROUPS, SG_CHUNK, SG_CHUNK), lambda c: (0, 0, 0)),
                  pl.BlockSpec((SG_CHUNK, SG_GROUPS), lambda c: (0, 0))],
        out_specs=pl.BlockSpec((SG_CHUNK, width), lambda c: (c, 0)),
        out_shape=jax.ShapeDtypeStruct((n, width), BF16),
        compiler_params=_params("parallel"),
        name="sgu",
    )(z, z, v_gain.reshape(1, width), w_s.astype(BF16), b_s.T)


def _pack_bf16_pair(lo, hi):
    lo_bits = lax.bitcast_convert_type(lo.astype(BF16).astype(F32), PACKED)
    hi_bits = lax.bitcast_convert_type(hi.astype(BF16).astype(F32), PACKED)
    return (hi_bits & jnp.uint32(0xFFFF0000)) | (lo_bits >> 16)


def _unpack_bf16_pair_f32(words):
    lo = lax.bitcast_convert_type(words << 16, F32)
    hi = lax.bitcast_convert_type(words & jnp.uint32(0xFFFF0000), F32)
    return lo, hi


def _unpack_bf16_pair(words):
    lo, hi = _unpack_bf16_pair_f32(words)
    return lo.astype(BF16), hi.astype(BF16)


def _router_body(x_ref, g_ref, wr_ref, br_ref, ids_ref, gates_ref, counts_ref, carry):
    @pl.when(pl.program_id(0) == 0)
    def _():
        carry[...] = jnp.zeros(carry.shape, carry.dtype)

    x = x_ref[...]
    ms = jnp.mean(x * x, axis=-1, keepdims=True)
    xn = x * lax.rsqrt(ms + RMS_EPS) * g_ref[...]
    xh = xn.astype(BF16)
    xl = (xn - xh.astype(F32)).astype(BF16)
    w = wr_ref[...]
    wh = w.astype(BF16)
    wl = (w - wh.astype(F32)).astype(BF16)
    hi = jnp.dot(xh, jnp.concatenate([wh, wl], axis=1), preferred_element_type=F32)
    lo = jnp.dot(xl, wh, preferred_element_type=F32)
    logits = (hi[:, :LANES] + (hi[:, LANES:] + lo)) + br_ref[...]
    lane = lax.broadcasted_iota(jnp.int32, logits.shape, 1)
    ninf = jnp.float32(-jnp.inf)

    lg = jnp.where(lane < N_GROUPS, logits, ninf)
    mg = jnp.max(lg, axis=-1, keepdims=True)
    grp = jnp.min(jnp.where(lg == mg, lane, LANES), axis=-1, keepdims=True)
    p_grp = 1.0 / jnp.sum(jnp.exp(lg - mg), axis=-1, keepdims=True)

    lo = N_GROUPS + grp * EXPERTS_PER_GROUP
    le = jnp.where((lane >= lo) & (lane < lo + EXPERTS_PER_GROUP), logits, ninf)
    m1 = jnp.max(le, axis=-1, keepdims=True)
    i1 = jnp.min(jnp.where(le == m1, lane, LANES), axis=-1, keepdims=True)
    le2 = jnp.where(lane == i1, ninf, le)
    m2 = jnp.max(le2, axis=-1, keepdims=True)
    i2 = jnp.min(jnp.where(le2 == m2, lane, LANES), axis=-1, keepdims=True)
    z = jnp.sum(jnp.exp(le - m1), axis=-1, keepdims=True)
    p1 = 1.0 / z
    p2 = jnp.exp(m2 - m1) / z
    tot = p1 + p2
    g1 = p_grp * (p1 / tot)
    g2 = p_grp * (p2 / tot)
    gates_ref[...] = jnp.where(lane == 0, g1, jnp.where(lane == 1, g2, 0.0))

    e0 = i1 - N_GROUPS
    e1 = i2 - N_GROUPS
    tm = x.shape[0]
    onehot = jnp.where((lane == e0) | (lane == e1), 1.0, 0.0)
    tri = jnp.where(lax.broadcasted_iota(jnp.int32, (tm, tm), 0) > lax.broadcasted_iota(jnp.int32, (tm, tm), 1),
                    1.0, 0.0).astype(BF16)
    before = jnp.dot(tri, onehot.astype(BF16), preferred_element_type=F32) + carry[...]
    r0 = jnp.sum(jnp.where(lane == e0, before, 0.0), axis=-1, keepdims=True).astype(jnp.int32)
    r1 = jnp.sum(jnp.where(lane == e1, before, 0.0), axis=-1, keepdims=True).astype(jnp.int32)
    carry[...] = carry[...] + jnp.sum(onehot, axis=0, keepdims=True)
    counts_ref[...] = carry[...].astype(jnp.int32)
    ids_ref[...] = jnp.where(lane == 0, e0, jnp.where(lane == 1, e1, jnp.where(lane == 2, r0, jnp.where(lane == 3, r1, 0))))


def _router(h, gain, w_group, b_group, w_expert, b_expert):
    n, d = h.shape
    tm = min(256, n)
    wr = jnp.concatenate([w_group, w_expert.transpose(1, 0, 2).reshape(d, N_EXPERTS)], axis=1)
    wr = jnp.pad(wr.astype(F32), ((0, 0), (0, LANES - wr.shape[1])))
    br = jnp.concatenate([b_group, b_expert.reshape(N_EXPERTS)]).astype(F32)
    br = jnp.pad(br, (0, LANES - br.shape[0])).reshape(1, LANES)
    return pl.pallas_call(
        _router_body,
        grid=(n // tm,),
        in_specs=[pl.BlockSpec((tm, d), lambda i: (i, 0)),
                  pl.BlockSpec((1, d), lambda i: (0, 0)),
                  pl.BlockSpec((d, LANES), lambda i: (0, 0)),
                  pl.BlockSpec((1, LANES), lambda i: (0, 0))],
        out_specs=[pl.BlockSpec((tm, LANES), lambda i: (i, 0)),
                   pl.BlockSpec((tm, LANES), lambda i: (i, 0)),
                   pl.BlockSpec((1, LANES), lambda i: (0, 0))],
        out_shape=[jax.ShapeDtypeStruct((n, LANES), jnp.int32),
                   jax.ShapeDtypeStruct((n, LANES), F32),
                   jax.ShapeDtypeStruct((1, LANES), jnp.int32)],
        scratch_shapes=[pltpu.VMEM((1, LANES), F32)],
        compiler_params=_params("arbitrary"),
        name="norm_router",
    )(h, gain.reshape(1, d), wr, br)


def _dispatch_plan(ids, counts):
    n = ids.shape[0]
    nka = n * 2
    nb = (nka + MOE_ROWS - 1) // MOE_ROWS + N_EXPERTS
    e, rank = ids[:, :2], ids[:, 2:4]
    blocks = (counts + MOE_ROWS - 1) // MOE_ROWS
    blk_end = jnp.cumsum(blocks)
    blk_start = blk_end - blocks
    n_used = blk_end[-1]
    lanes = jnp.arange(N_EXPERTS, dtype=jnp.int32)
    row_start = jnp.sum(jnp.where(e[:, :, None] == lanes, blk_start * MOE_ROWS, 0), axis=-1)
    dest = (row_start + rank).astype(jnp.int32).reshape(nka)
    b = jnp.arange(nb, dtype=jnp.int32)
    block_e = jnp.minimum(jnp.sum(b[:, None] >= blk_end[None, :], axis=1), N_EXPERTS - 1).astype(jnp.int32)
    first = ((b == blk_start[block_e]) & (b < n_used)).astype(jnp.int32)
    nxt_blk = blk_end[block_e]
    nxt_e = jnp.where(nxt_blk < n_used, block_e[jnp.minimum(nxt_blk, nb - 1)], -1).astype(jnp.int32)
    return dest, block_e, first, nxt_e, n_used.astype(jnp.int32).reshape(1), nb


def _dispatch_body(dest_ref, h_ref, g_ref, xb_any, xb_hbm, buf, sem):
    del xb_any
    i = pl.program_id(0)
    n_steps = pl.num_programs(0)
    tm = h_ref.shape[0]
    slot = i % 2

    def row_copies(step, s, wait_only=False):
        for r in range(tm):
            for k in range(2):
                row = 0 if wait_only else dest_ref[(step * tm + r) * 2 + k]
                yield _row_copy(buf.at[s], r, xb_hbm, row, sem.at[s])

    @pl.when(i >= 2)
    def _():
        for c in row_copies(0, slot, wait_only=True):
            c.wait()

    x = h_ref[...]
    ms = jnp.mean(x * x, axis=-1, keepdims=True)
    xn = x * lax.rsqrt(ms + RMS_EPS) * g_ref[...]
    half_d = xn.shape[1] // 2
    buf[slot] = _pack_bf16_pair(xn[:, :half_d], xn[:, half_d:])
    for c in row_copies(i, slot):
        c.start()

    @pl.when(i == n_steps - 1)
    def _():
        for c in row_copies(0, slot, wait_only=True):
            c.wait()

        @pl.when(n_steps >= 2)
        def _():
            for c in row_copies(0, 1 - slot, wait_only=True):
                c.wait()


def _dispatch(h, gain, dest, xb_prev):
    n, d = h.shape
    tm = min(128, n)
    grid_spec = pltpu.PrefetchScalarGridSpec(
        num_scalar_prefetch=1,
        grid=(n // tm,),
        in_specs=[pl.BlockSpec((tm, d), lambda i, dest_: (i, 0)),
                  pl.BlockSpec((1, d), lambda i, dest_: (0, 0)),
                  pl.BlockSpec(memory_space=pl.ANY)],
        out_specs=pl.BlockSpec(memory_space=pl.ANY),
        scratch_shapes=[pltpu.VMEM((2, tm, d // 2), PACKED),
                        pltpu.SemaphoreType.DMA((2,))],
    )
    return pl.pallas_call(
        _dispatch_body,
        grid_spec=grid_spec,
        out_shape=jax.ShapeDtypeStruct(xb_prev.shape, PACKED),
        input_output_aliases={3: 0},
        compiler_params=_params("arbitrary"),
        name="moe_dispatch",
    )(dest, h, gain.reshape(1, d), xb_prev)


def _row_copy(src, src_row, dst, dst_row, sem):
    return pltpu.make_async_copy(src.at[pl.ds(src_row, 1), :], dst.at[pl.ds(dst_row, 1), :], sem)


def _experts_body(be_ref, first_ref, nxt_ref, nu_ref, x_ref, w1_hbm, w3_hbm, w2_hbm, o_ref,
                  st13, st2, wb13, wb2, wsem, *, layer):
    i = pl.program_id(0)
    n_used = nu_ref[0]
    rows = x_ref.shape[0]
    de = st2.shape[0]

    def weight_copies(e):
        return (pltpu.make_async_copy(w1_hbm.at[layer, e], st13.at[0], wsem.at[0]),
                pltpu.make_async_copy(w3_hbm.at[layer, e], st13.at[1], wsem.at[1]),
                pltpu.make_async_copy(w2_hbm.at[layer, e], st2, wsem.at[2]))

    @pl.when(i == 0)
    def _():
        for c in weight_copies(be_ref[0]):
            c.start()

    @pl.when(i < n_used)
    def _():
        @pl.when(first_ref[i] == 1)
        def _():
            for c in weight_copies(0):
                c.wait()
            wb13[:, :de] = st13[0].astype(BF16)
            wb13[:, de:] = st13[1].astype(BF16)
            wb2[...] = st2[...].astype(BF16)

            @pl.when(nxt_ref[i] >= 0)
            def _():
                for c in weight_copies(nxt_ref[i]):
                    c.start()

        x = jnp.concatenate(_unpack_bf16_pair(x_ref[...]), axis=1)
        half = rows // 2
        wide = (2 * de) // (2 * MXU_TILE) * (2 * MXU_TILE)
        h_a = jnp.dot(x, wb13[:, :wide], preferred_element_type=F32)
        h_b = jnp.concatenate(
            [jnp.dot(x[:half], wb13[:, wide:], preferred_element_type=F32),
             jnp.dot(x[half:], wb13[:, wide:], preferred_element_type=F32)], axis=0)
        h13 = jnp.concatenate([h_a, h_b], axis=1)
        h1, h3 = h13[:, :de], h13[:, de:]
        a = ((h1 / (1.0 + jnp.exp(-h1))) * h3).astype(BF16)
        half_d = wb2.shape[1] // 2
        cols = half_d // 2
        for c in range(2):
            lo = jnp.dot(a, wb2[:, c * cols:(c + 1) * cols], preferred_element_type=F32)
            hi = jnp.dot(a, wb2[:, half_d + c * cols:half_d + (c + 1) * cols], preferred_element_type=F32)
            o_ref[:, c * cols:(c + 1) * cols] = _pack_bf16_pair(lo, hi)

    @pl.when(i >= n_used)
    def _():
        o_ref[...] = jnp.zeros(o_ref.shape, o_ref.dtype)


def _experts(xb, block_e, first, nxt_e, n_used, nb, w1, w3, w2, *, layer):
    d, de = w1.shape[2:]
    assert xb.shape == (nb * MOE_ROWS, d // 2)
    grid_spec = pltpu.PrefetchScalarGridSpec(
        num_scalar_prefetch=4,
        grid=(nb,),
        in_specs=[pl.BlockSpec((MOE_ROWS, d // 2), lambda i, *_: (i, 0))] + [pl.BlockSpec(memory_space=pl.ANY)] * 3,
        out_specs=pl.BlockSpec((MOE_ROWS, d // 2), lambda i, *_: (i, 0)),
        scratch_shapes=[pltpu.VMEM((2, d, de), F32),
                        pltpu.VMEM((de, d), F32),
                        pltpu.VMEM((d, 2 * de), BF16),
                        pltpu.VMEM((de, d), BF16),
                        pltpu.SemaphoreType.DMA((3,))],
    )
    return pl.pallas_call(
        functools.partial(_experts_body, layer=layer),
        grid_spec=grid_spec,
        out_shape=jax.ShapeDtypeStruct((nb * MOE_ROWS, d // 2), PACKED),
        compiler_params=pltpu.CompilerParams(dimension_semantics=("arbitrary",),
                                             vmem_limit_bytes=EXPERTS_VMEM_LIMIT),
        name="experts",
    )(block_e, first, nxt_e, n_used, xb, w1, w3, w2)


def _combine_body(pos_ref, h_ref, g_ref, gain_ref, y_hbm, *rest, last):
    if last:
        n_ref, ybuf, sem = rest
    else:
        o_ref, n_ref, ybuf, sem = rest
    i = pl.program_id(0)
    n_steps = pl.num_programs(0)
    tm = h_ref.shape[0]

    def start_gather(step, slot):
        for r in range(tm):
            base = (step * tm + r) * 2
            _row_copy(y_hbm, pos_ref[base], ybuf.at[slot, 0], r, sem.at[slot]).start()
            _row_copy(y_hbm, pos_ref[base + 1], ybuf.at[slot, 1], r, sem.at[slot]).start()

    def wait_gather(slot):
        for r in range(tm):
            _row_copy(y_hbm, 0, ybuf.at[slot, 0], r, sem.at[slot]).wait()
            _row_copy(y_hbm, 0, ybuf.at[slot, 1], r, sem.at[slot]).wait()

    slot = i % 2

    @pl.when(i == 0)
    def _():
        start_gather(0, 0)

    @pl.when(i + 1 < n_steps)
    def _():
        start_gather(i + 1, 1 - slot)

    wait_gather(slot)
    g = g_ref[...]
    lo0, hi0 = _unpack_bf16_pair_f32(ybuf[slot, 0])
    lo1, hi1 = _unpack_bf16_pair_f32(ybuf[slot, 1])
    moe = jnp.concatenate([g[:, 0:1] * lo0 + g[:, 1:2] * lo1, g[:, 0:1] * hi0 + g[:, 1:2] * hi1], axis=1)
    out = h_ref[...] + moe
    if not last:
        o_ref[...] = out
    ms = jnp.mean(out * out, axis=-1, keepdims=True)
    n_ref[...] = (out * lax.rsqrt(ms + RMS_EPS) * gain_ref[...]).astype(n_ref.dtype)


def _combine(h, gates, pos, yb, next_gain, *, last):
    n, d = h.shape
    tm = min(128, n)
    row_spec = pl.BlockSpec((tm, d), lambda i, pos_: (i, 0))
    grid_spec = pltpu.PrefetchScalarGridSpec(
        num_scalar_prefetch=1,
        grid=(n // tm,),
        in_specs=[row_spec,
                  pl.BlockSpec((tm, LANES), lambda i, pos_: (i, 0)),
                  pl.BlockSpec((1, d), lambda i, pos_: (0, 0)),
                  pl.BlockSpec(memory_space=pl.ANY)],
        out_specs=[row_spec] if last else [row_spec, row_spec],
        scratch_shapes=[pltpu.VMEM((2, 2, tm, d // 2), PACKED),
                        pltpu.SemaphoreType.DMA((2,))],
    )
    normed = jax.ShapeDtypeStruct((n, d), F32 if last else BF16)
    return pl.pallas_call(
        functools.partial(_combine_body, last=last),
        grid_spec=grid_spec,
        out_shape=[normed] if last else [jax.ShapeDtypeStruct((n, d), F32), normed],
        compiler_params=_params("arbitrary"),
        name="moe_combine",
    )(pos, h, gates, next_gain.reshape(1, d), yb)


def _hierarchical_moe(h, gain, w_group, b_group, w_expert, b_expert, w1, w3, w2, next_gain, xb_prev, *, layer, last):
    ids, gates, counts = _router(h, gain, w_group, b_group, w_expert, b_expert)
    dest, block_e, first, nxt_e, n_used, nb = _dispatch_plan(ids[:, :4], counts[0, :N_EXPERTS])
    if xb_prev is None:
        xb_prev = jnp.zeros((nb * MOE_ROWS, h.shape[1] // 2), PACKED)
    xb = _dispatch(h, gain, dest, xb_prev)
    yb = _experts(xb, block_e, first, nxt_e, n_used, nb, w1, w3, w2, layer=layer)
    return _combine(h, gates, dest, yb, next_gain, last=last), xb


def kernel(x, norm_mix, norm_ffn, norm_final, a_w_in, a_q_gain, a_k_gain, a_w_o, b_w_in, b_v_gain, b_w_s, b_bias, b_w_o, c_w_in, c_sink, c_w_o, d_w_in, d_rpb, d_w_o, moe_w_group, moe_b_group, moe_w_expert, moe_b_expert, moe_w1, moe_w3, moe_w2):
    batch, seq, d = x.shape
    h = x.reshape(batch * seq, d)
    ones = jnp.ones((HEAD_DIM,), F32)
    depth = norm_mix.shape[0]
    hn = _rmsnorm(h, norm_mix[0], BF16)
    xb = None
    for i in range(depth):
        m, j = i % N_MIXERS, i // N_MIXERS
        if m == 0:
            cos, sin = _axial_tables(seq)
            qkv = _matmul_qkv(hn, a_w_in[j], cos, sin, a_q_gain[j], a_k_gain[j],
                              mode="axial", seq=seq)
            mixed, w_o = _attn_dense(qkv, seq), a_w_o[j]
        elif m == 1:
            z = _matmul_gelu(hn, b_w_in[j], F32)
            mixed, w_o = _sgu(z, b_v_gain[j], b_w_s[j], b_bias[j]), b_w_o[j]
        elif m == 2:
            cos, sin = _rope_tables(seq)
            qkv = _matmul_qkv(hn, c_w_in[j], cos, sin, ones, ones, mode="rope", seq=seq)
            mixed, w_o = _attn_window(qkv, c_sink[j].astype(F32), seq), c_w_o[j]
        else:
            cos, sin = _rope_tables(seq)
            qkv = _matmul_qkv(hn, d_w_in[j], cos, sin, ones, ones, mode="none", seq=seq)
            mixed, w_o = _attn_nbr(qkv, d_rpb[j], seq), d_w_o[j]
        h = _matmul_residual(mixed, w_o, h)
        last = i == depth - 1
        outs, xb = _hierarchical_moe(h, norm_ffn[i], moe_w_group[i], moe_b_group[i], moe_w_expert[i],
                                     moe_b_expert[i], moe_w1, moe_w3, moe_w2,
                                     norm_final if last else norm_mix[i + 1], xb, layer=i, last=last)
        if last:
            return outs[0].reshape(batch, seq, d)
        h, hn = outs
```

```python
import functools

import jax
import jax.numpy as jnp
from jax import lax
from jax.experimental import pallas as pl
from jax.experimental.pallas import tpu as pltpu

D_MODEL = 4096
BATCH = 2
SEQ = 4096
DEPTH = 4
N_MIXERS = 4
HEAD_DIM = 128
GQA_GROUP = 4
ROPE_THETA = 10000.0
Q_BLOCK = 128
GRID_W = 64
SG_CHUNK = 128
SG_GROUPS = 8
WINDOW = 128
NA_ROWS = 8
NA_COLS = 16
N_GROUPS = 8
EXPERTS_PER_GROUP = 4
N_EXPERTS = N_GROUPS * EXPERTS_PER_GROUP
D_EXPERT = 384
MOE_ROWS = 256
RMS_EPS = 1e-6
NEG_INF = -1e30

LOG2_E = 1.4426950408889634
LANES = 128
MXU_TILE = 256
DENSE_KEY_CHUNK = 1024
DENSE_Q_ROWS = 256
NA_WIN_ROWS = 10
WINDOW_Q_ROWS = 256
WINDOW_BLOCKS_PER_STEP = 8
NA_BLOCKS_PER_STEP = 8
VMEM_LIMIT = 48 * 1024 * 1024
EXPERTS_VMEM_LIMIT = 56 * 1024 * 1024

F32 = jnp.float32
BF16 = jnp.bfloat16
PACKED = jnp.uint32


def _params(*sem):
    return pltpu.CompilerParams(dimension_semantics=sem, vmem_limit_bytes=VMEM_LIMIT)


def _rmsnorm_body(x_ref, g_ref, o_ref):
    x = x_ref[...]
    ms = jnp.mean(x * x, axis=-1, keepdims=True)
    o_ref[...] = (x * lax.rsqrt(ms + RMS_EPS) * g_ref[...]).astype(o_ref.dtype)


def _rmsnorm(x, gain, out_dtype):
    n, d = x.shape
    tm = min(256, n)
    return pl.pallas_call(
        _rmsnorm_body,
        grid=(n // tm,),
        in_specs=[pl.BlockSpec((tm, d), lambda i: (i, 0)),
                  pl.BlockSpec((1, d), lambda i: (0, 0))],
        out_specs=pl.BlockSpec((tm, d), lambda i: (i, 0)),
        out_shape=jax.ShapeDtypeStruct((n, d), out_dtype),
        compiler_params=_params("parallel"),
        name="rmsnorm",
    )(x, gain.reshape(1, d))


def _gelu_tanh(x):
    return 0.5 * x * (1.0 + jnp.tanh(0.7978845608028654 * (x + 0.044715 * (x * x * x))))


def _mm_gelu_body(x_ref, w_ref, o_ref):
    acc = jnp.dot(x_ref[...], w_ref[...].astype(BF16), preferred_element_type=F32)
    o_ref[...] = _gelu_tanh(acc).astype(o_ref.dtype)


def _mm_res_body(x_ref, w_ref, r_ref, o_ref):
    acc = jnp.dot(x_ref[...], w_ref[...].astype(BF16), preferred_element_type=F32)
    o_ref[...] = r_ref[...] + acc


def _swap_halves(y, width):
    if 2 * width == LANES:
        return pltpu.roll(y, width, axis=1)
    lane = lax.broadcasted_iota(jnp.int32, y.shape, 1)
    first = (lane % (2 * width)) < width
    return jnp.where(first, pltpu.roll(y, LANES - width, axis=1), pltpu.roll(y, width, axis=1))


def _mm_qkv_body(x_ref, w_ref, cos_ref, sin_ref, gq_ref, gk_ref, o_ref, *, mode, n_q, n_qk, scale):
    j = pl.program_id(1)
    is_q = j < n_q
    is_qk = j < n_qk
    x = x_ref[...]
    tn = o_ref.shape[1]

    def head_post(a):
        if mode == "axial":
            gain = jnp.where(is_q, gq_ref[...] * scale, gk_ref[...])
            a = a * lax.rsqrt(jnp.mean(a * a, axis=-1, keepdims=True) + RMS_EPS)
        else:
            gain = jnp.where(is_q, jnp.full((1, HEAD_DIM), scale, F32), jnp.ones((1, HEAD_DIM), F32))
        a = a * gain
        if mode == "axial":
            a = a * cos_ref[...] + _swap_halves(a, HEAD_DIM // 4) * sin_ref[...]
        elif mode == "rope":
            a = a * cos_ref[...] + _swap_halves(a, HEAD_DIM // 2) * sin_ref[...]
        return a

    if mode == "axial":
        acc = jnp.dot(x, w_ref[...].astype(BF16), preferred_element_type=F32)

        @pl.when(is_qk)
        def _():
            for t in range(tn // HEAD_DIM):
                o_ref[:, t * HEAD_DIM:(t + 1) * HEAD_DIM] = head_post(
                    acc[:, t * HEAD_DIM:(t + 1) * HEAD_DIM]).astype(o_ref.dtype)

        @pl.when(jnp.logical_not(is_qk))
        def _():
            o_ref[...] = acc.astype(o_ref.dtype)
    else:
        piece = min(MXU_TILE, tn)
        for pc in range(tn // piece):
            acc = jnp.dot(x, w_ref[:, pc * piece:(pc + 1) * piece].astype(BF16), preferred_element_type=F32)
            for t in range(piece // HEAD_DIM):
                raw = acc[:, t * HEAD_DIM:(t + 1) * HEAD_DIM]
                lo = pc * piece + t * HEAD_DIM
                o_ref[:, lo:lo + HEAD_DIM] = jnp.where(is_qk, head_post(raw), raw).astype(o_ref.dtype)


def _mm_tiles(m, n):
    return min(1024, m), min(512, n)


def _matmul_gelu(x, w, out_dtype):
    m, k = x.shape
    n = w.shape[1]
    tm, tn = _mm_tiles(m, n)
    return pl.pallas_call(
        _mm_gelu_body,
        grid=(m // tm, n // tn),
        in_specs=[pl.BlockSpec((tm, k), lambda i, j: (i, 0)),
                  pl.BlockSpec((k, tn), lambda i, j: (0, j))],
        out_specs=pl.BlockSpec((tm, tn), lambda i, j: (i, j)),
        out_shape=jax.ShapeDtypeStruct((m, n), out_dtype),
        compiler_params=_params("parallel", "parallel"),
        name="matmul_gelu",
    )(x, w)


def _matmul_residual(x, w, res):
    m, k = x.shape
    n = w.shape[1]
    tm, tn = _mm_tiles(m, n)
    return pl.pallas_call(
        _mm_res_body,
        grid=(m // tm, n // tn),
        in_specs=[pl.BlockSpec((tm, k), lambda i, j: (i, 0)),
                  pl.BlockSpec((k, tn), lambda i, j: (0, j)),
                  pl.BlockSpec((tm, tn), lambda i, j: (i, j))],
        out_specs=pl.BlockSpec((tm, tn), lambda i, j: (i, j)),
        out_shape=jax.ShapeDtypeStruct((m, n), F32),
        compiler_params=_params("parallel", "parallel"),
        name="matmul_residual",
    )(x, w, res)


def _matmul_qkv(x, w, cos, sin, gq, gk, *, mode, seq):
    m, k = x.shape
    n = w.shape[1]
    d_q = D_MODEL
    d_kv = (n - d_q) // 2
    tm = min(1024, seq)
    tn = min(512, d_kv)
    s_blocks = seq // tm
    scale = HEAD_DIM ** -0.5 * LOG2_E
    body = functools.partial(_mm_qkv_body, mode=mode, n_q=d_q // tn, n_qk=(d_q + d_kv) // tn, scale=scale)
    return pl.pallas_call(
        body,
        grid=(m // tm, n // tn),
        in_specs=[pl.BlockSpec((tm, k), lambda i, j: (i, 0)),
                  pl.BlockSpec((k, tn), lambda i, j: (0, j)),
                  pl.BlockSpec((tm, HEAD_DIM), lambda i, j: (i % s_blocks, 0)),
                  pl.BlockSpec((tm, HEAD_DIM), lambda i, j: (i % s_blocks, 0)),
                  pl.BlockSpec((1, HEAD_DIM), lambda i, j: (0, 0)),
                  pl.BlockSpec((1, HEAD_DIM), lambda i, j: (0, 0))],
        out_specs=pl.BlockSpec((tm, tn), lambda i, j: (i, j)),
        out_shape=jax.ShapeDtypeStruct((m, n), BF16),
        compiler_params=_params("parallel", "parallel"),
        name="matmul_qkv_" + mode,
    )(x, w, cos, sin, gq.reshape(1, HEAD_DIM), gk.reshape(1, HEAD_DIM))


def _rope_cos_sin(pos, dim):
    half = dim // 2
    inv = jnp.power(jnp.float32(ROPE_THETA), -jnp.arange(half, dtype=F32) * (2.0 / dim))
    ang = pos.astype(F32)[:, None] * inv[None, :]
    return jnp.cos(ang), jnp.sin(ang)


def _axial_tables(seq):
    t = jnp.arange(seq, dtype=jnp.int32)
    cr, sr = _rope_cos_sin(t // GRID_W, HEAD_DIM // 2)
    cc, sc = _rope_cos_sin(t % GRID_W, HEAD_DIM // 2)
    return (jnp.concatenate([cr, cr, cc, cc], axis=-1),
            jnp.concatenate([-sr, sr, -sc, sc], axis=-1))


def _rope_tables(seq):
    c, s = _rope_cos_sin(jnp.arange(seq, dtype=jnp.int32), HEAD_DIM)
    return jnp.concatenate([c, c], axis=-1), jnp.concatenate([-s, s], axis=-1)


def _stack_heads(q_ref, row0=0, rows=None):
    rows = q_ref.shape[0] if rows is None else rows
    return jnp.concatenate(
        [q_ref[row0:row0 + rows, g * HEAD_DIM:(g + 1) * HEAD_DIM] for g in range(GQA_GROUP)], axis=0)


def _unstack_heads(o, o_ref, row0=0, rows=None):
    rows = o_ref.shape[0] if rows is None else rows
    for g in range(GQA_GROUP):
        o_ref[row0:row0 + rows, g * HEAD_DIM:(g + 1) * HEAD_DIM] = o[g * rows:(g + 1) * rows].astype(o_ref.dtype)


def _qk(q, k):
    return lax.dot_general(q, k, (((1,), (1,)), ((), ())), preferred_element_type=F32)


def _attn_dense_body(q_ref, k_ref, v_ref, o_ref):
    q = _stack_heads(q_ref)
    seq = k_ref.shape[0]
    kc = min(DENSE_KEY_CHUNK, seq)
    m = l = acc = None
    for c in range(seq // kc):
        s = _qk(q, k_ref[c * kc:(c + 1) * kc, :])
        mc = jnp.max(s, axis=-1, keepdims=True)
        if c == 0:
            m = mc
            p = jnp.exp2(s - m)
            l = jnp.sum(p, axis=-1, keepdims=True)
            acc = jnp.dot(p.astype(BF16), v_ref[c * kc:(c + 1) * kc, :], preferred_element_type=F32)
        else:
            m_new = jnp.maximum(m, mc)
            alpha = jnp.exp2(m - m_new)
            p = jnp.exp2(s - m_new)
            l = alpha * l + jnp.sum(p, axis=-1, keepdims=True)
            acc = alpha * acc + jnp.dot(p.astype(BF16), v_ref[c * kc:(c + 1) * kc, :],
                                        preferred_element_type=F32)
            m = m_new
    _unstack_heads(acc / l, o_ref)


def _attn_dense(qkv, seq):
    n = qkv.shape[0]
    batch = n // seq
    n_kv = D_MODEL // (GQA_GROUP * HEAD_DIM)
    k_off = D_MODEL // HEAD_DIM
    v_off = k_off + n_kv
    tq = DENSE_Q_ROWS
    nq = seq // tq
    gw = GQA_GROUP * HEAD_DIM
    return pl.pallas_call(
        _attn_dense_body,
        grid=(batch, n_kv, nq),
        in_specs=[pl.BlockSpec((tq, gw), lambda b, h, i: (b * nq + i, h)),
                  pl.BlockSpec((seq, HEAD_DIM), lambda b, h, i: (b, k_off + h)),
                  pl.BlockSpec((seq, HEAD_DIM), lambda b, h, i: (b, v_off + h))],
        out_specs=pl.BlockSpec((tq, gw), lambda b, h, i: (b * nq + i, h)),
        out_shape=jax.ShapeDtypeStruct((n, D_MODEL), BF16),
        compiler_params=_params("parallel", "parallel", "parallel"),
        name="attn_dense",
    )(qkv, qkv, qkv)


def _attn_window_body(sink_ref, q_ref, k_ref, v_ref, o_ref):
    h = pl.program_id(1)
    tq = WINDOW_Q_ROWS
    nkeys = tq + 2 * WINDOW
    shape = (GQA_GROUP * tq, nkeys)
    rel = (lax.broadcasted_iota(jnp.int32, shape, 0) % tq
           - lax.broadcasted_iota(jnp.int32, shape, 1) + WINDOW)
    rowg = lax.broadcasted_iota(jnp.int32, (shape[0], 1), 0) // tq
    sink = jnp.zeros((shape[0], 1), F32)
    for g in range(GQA_GROUP):
        sink = jnp.where(rowg == g, sink_ref[h * GQA_GROUP + g] * LOG2_E, sink)
    for sub in range(q_ref.shape[0] // tq):
        i = pl.program_id(2) * (q_ref.shape[0] // tq) + sub
        start = pl.multiple_of(jnp.clip(i * tq - WINDOW, 0, k_ref.shape[0] - nkeys), WINDOW)
        k3 = k_ref[pl.ds(start, nkeys), :]
        v3 = v_ref[pl.ds(start, nkeys), :]
        s = _qk(_stack_heads(q_ref, sub * tq, tq), k3)
        in_window = (rel + (i * tq - start)).astype(jnp.uint32) <= 2 * WINDOW
        s = jnp.where(in_window, s, NEG_INF)
        m = jnp.maximum(jnp.max(s, axis=-1, keepdims=True), sink)
        p = jnp.exp2(s - m)
        denom = jnp.sum(p, axis=-1, keepdims=True) + jnp.exp2(sink - m)
        o = jnp.dot(p.astype(BF16), v3, preferred_element_type=F32) / denom
        _unstack_heads(o, o_ref, sub * tq, tq)


def _attn_window(qkv, sink, seq):
    n = qkv.shape[0]
    batch = n // seq
    n_kv = D_MODEL // (GQA_GROUP * HEAD_DIM)
    k_off = D_MODEL // HEAD_DIM
    v_off = k_off + n_kv
    tq = min(WINDOW_BLOCKS_PER_STEP * WINDOW_Q_ROWS, seq)
    nq = seq // tq
    gw = GQA_GROUP * HEAD_DIM
    assert seq >= WINDOW_Q_ROWS + 2 * WINDOW
    grid_spec = pltpu.PrefetchScalarGridSpec(
        num_scalar_prefetch=1,
        grid=(batch, n_kv, nq),
        in_specs=[pl.BlockSpec((tq, gw), lambda b, h, i, s_: (b * nq + i, h)),
                  pl.BlockSpec((seq, HEAD_DIM), lambda b, h, i, s_: (b, k_off + h)),
                  pl.BlockSpec((seq, HEAD_DIM), lambda b, h, i, s_: (b, v_off + h))],
        out_specs=pl.BlockSpec((tq, gw), lambda b, h, i, s_: (b * nq + i, h)),
    )
    return pl.pallas_call(
        _attn_window_body,
        grid_spec=grid_spec,
        out_shape=jax.ShapeDtypeStruct((n, D_MODEL), BF16),
        compiler_params=_params("parallel", "parallel", "parallel"),
        name="attn_window",
    )(sink, qkv, qkv, qkv)


def _na_window_start(i, rows):
    return jnp.clip(2 * i - NA_ROWS // 2, 0, rows - NA_WIN_ROWS)


NA_DR = 2 * NA_ROWS - 1
NA_DC = 2 * NA_COLS - 1
NA_PAIR_LO = -2
NA_PAIRS = NA_DR + 3


def _na_table_body(rpb_ref, o_ref):
    h = pl.program_id(0)
    shape = (GRID_W, 2 * GRID_W)
    c = lax.broadcasted_iota(jnp.int32, shape, 0)
    lane = lax.broadcasted_iota(jnp.int32, shape, 1)
    kc = lane % GRID_W
    d = kc - c + (NA_COLS - 1)
    cs = jnp.clip(c - NA_COLS // 2, 0, GRID_W - NA_COLS)
    col_ok = (kc >= cs) & (kc < cs + NA_COLS)
    neg = jnp.full(shape, NEG_INF, F32)
    rows = []
    for a in range(NA_DR):
        val = jnp.zeros(shape, F32)
        for b in range(NA_DC):
            val = jnp.where(d == b, rpb_ref[(h * NA_DR + a) * NA_DC + b] * LOG2_E, val)
        rows.append(jnp.where(col_ok, val, neg))
    for e in range(NA_PAIRS):
        a0 = e + NA_PAIR_LO
        lo = rows[a0] if 0 <= a0 < NA_DR else neg
        hi = rows[a0 + 1] if 0 <= a0 + 1 < NA_DR else neg
        o_ref[0, e] = jnp.where(lane >= GRID_W, hi, lo)


def _na_table(rpb):
    n_heads = rpb.shape[0]
    grid_spec = pltpu.PrefetchScalarGridSpec(
        num_scalar_prefetch=1,
        grid=(n_heads,),
        in_specs=[],
        out_specs=pl.BlockSpec((1, NA_PAIRS, GRID_W, 2 * GRID_W), lambda h, r: (h, 0, 0, 0)),
    )
    return pl.pallas_call(
        _na_table_body,
        grid_spec=grid_spec,
        out_shape=jax.ShapeDtypeStruct((n_heads, NA_PAIRS, GRID_W, 2 * GRID_W), F32),
        compiler_params=_params("arbitrary"),
        name="na_bias_table",
    )(rpb.astype(F32).reshape(-1))


def _attn_nbr_body(q_ref, k_ref, v_ref, t_ref, o_ref, *, rows):
    n_sub = q_ref.shape[0] // Q_BLOCK
    for sub in range(n_sub):
        i = pl.program_id(2) * n_sub + sub
        ws = _na_window_start(i, rows)
        start = pl.multiple_of(ws * GRID_W, GRID_W)
        nkeys = NA_WIN_ROWS * GRID_W
        kw = k_ref[pl.ds(start, nkeys), :]
        vw = v_ref[pl.ds(start, nkeys), :]
        lane = lax.broadcasted_iota(jnp.int32, (1, 2 * GRID_W), 1)
        blocks = [[] for _ in range(GQA_GROUP)]
        for u in range(Q_BLOCK // GRID_W):
            r = (Q_BLOCK // GRID_W) * i + u
            rs = jnp.clip(r - NA_ROWS // 2, 0, rows - NA_ROWS)
            tiles = [[] for _ in range(GQA_GROUP)]
            for jp in range(NA_WIN_ROWS // 2):
                kr = ws + 2 * jp
                ok0 = (kr >= rs) & (kr < rs + NA_ROWS)
                ok1 = (kr + 1 >= rs) & (kr + 1 < rs + NA_ROWS)
                pen = jnp.where(lane < GRID_W, jnp.where(ok0, 0.0, NEG_INF), jnp.where(ok1, 0.0, NEG_INF))
                e = kr - r + (NA_ROWS - 1) - NA_PAIR_LO
                for g in range(GQA_GROUP):
                    tiles[g].append(t_ref[g, e] + pen)
            for g in range(GQA_GROUP):
                blocks[g].append(jnp.concatenate(tiles[g], axis=1))
        bias = jnp.concatenate([b for g in range(GQA_GROUP) for b in blocks[g]], axis=0)
        s = _qk(_stack_heads(q_ref, sub * Q_BLOCK, Q_BLOCK), kw) + bias
        m = jnp.max(s, axis=-1, keepdims=True)
        p = jnp.exp2(s - m)
        l = jnp.sum(p, axis=-1, keepdims=True)
        o = jnp.dot(p.astype(BF16), vw, preferred_element_type=F32) / l
        _unstack_heads(o, o_ref, sub * Q_BLOCK, Q_BLOCK)


def _attn_nbr(qkv, rpb, seq):
    n = qkv.shape[0]
    batch = n // seq
    n_kv = D_MODEL // (GQA_GROUP * HEAD_DIM)
    k_off = D_MODEL // HEAD_DIM
    v_off = k_off + n_kv
    tq = min(NA_BLOCKS_PER_STEP * Q_BLOCK, seq)
    nq = seq // tq
    gw = GQA_GROUP * HEAD_DIM
    rows = seq // GRID_W
    assert rows >= NA_WIN_ROWS and Q_BLOCK == 2 * GRID_W and rpb.shape[1:] == (NA_DR, NA_DC)
    table = _na_table(rpb)
    return pl.pallas_call(
        functools.partial(_attn_nbr_body, rows=rows),
        grid=(n_kv, batch, nq),
        in_specs=[pl.BlockSpec((tq, gw), lambda h, b, i: (b * nq + i, h)),
                  pl.BlockSpec((seq, HEAD_DIM), lambda h, b, i: (b, k_off + h)),
                  pl.BlockSpec((seq, HEAD_DIM), lambda h, b, i: (b, v_off + h)),
                  pl.BlockSpec((GQA_GROUP, NA_PAIRS, GRID_W, 2 * GRID_W), lambda h, b, i: (h, 0, 0, 0))],
        out_specs=pl.BlockSpec((tq, gw), lambda h, b, i: (b * nq + i, h)),
        out_shape=jax.ShapeDtypeStruct((n, D_MODEL), BF16),
        compiler_params=_params("parallel", "parallel", "parallel"),
        name="attn_nbr",
    )(qkv, qkv, qkv, table)


def _sgu_body(u_ref, v_ref, vg_ref, ws_ref, bs_ref, o_ref):
    v = v_ref[...]
    ms = jnp.mean(v * v, axis=-1, keepdims=True)
    vn = (v * lax.rsqrt(ms + RMS_EPS) * vg_ref[...]).astype(BF16)
    cg = v.shape[1] // SG_GROUPS
    for g in range(SG_GROUPS):
        mixed = jnp.dot(ws_ref[g], vn[:, g * cg:(g + 1) * cg], preferred_element_type=F32)
        mixed = mixed + bs_ref[:, g:g + 1]
        o_ref[:, g * cg:(g + 1) * cg] = (u_ref[:, g * cg:(g + 1) * cg] * mixed).astype(o_ref.dtype)


def _sgu(z, v_gain, w_s, b_s):
    n = z.shape[0]
    width = z.shape[1] // 2
    return pl.pallas_call(
        _sgu_body,
        grid=(n // SG_CHUNK,),
        in_specs=[pl.BlockSpec((SG_CHUNK, width), lambda c: (c, 0)),
                  pl.BlockSpec((SG_CHUNK, width), lambda c: (c, 1)),
                  pl.BlockSpec((1, width), lambda c: (0, 0)),
                  pl.BlockSpec((SG_GROUPS, SG_CHUNK, SG_CHUNK), lambda c: (0, 0, 0)),
                  pl.BlockSpec((SG_CHUNK, SG_GROUPS), lambda c: (0, 0))],
        out_specs=pl.BlockSpec((SG_CHUNK, width), lambda c: (c, 0)),
        out_shape=jax.ShapeDtypeStruct((n, width), BF16),
        compiler_params=_params("parallel"),
        name="sgu",
    )(z, z, v_gain.reshape(1, width), w_s.astype(BF16), b_s.T)


def _pack_bf16_pair(lo, hi):
    lo_bits = lax.bitcast_convert_type(lo.astype(BF16).astype(F32), PACKED)
    hi_bits = lax.bitcast_convert_type(hi.astype(BF16).astype(F32), PACKED)
    return (hi_bits & jnp.uint32(0xFFFF0000)) | (lo_bits >> 16)


def _unpack_bf16_pair_f32(words):
    lo = lax.bitcast_convert_type(words << 16, F32)
    hi = lax.bitcast_convert_type(words & jnp.uint32(0xFFFF0000), F32)
    return lo, hi


def _unpack_bf16_pair(words):
    lo, hi = _unpack_bf16_pair_f32(words)
    return lo.astype(BF16), hi.astype(BF16)


def _router_body(x_ref, g_ref, wr_ref, br_ref, ids_ref, gates_ref, counts_ref, carry):
    @pl.when(pl.program_id(0) == 0)
    def _():
        carry[...] = jnp.zeros(carry.shape, carry.dtype)

    x = x_ref[...]
    ms = jnp.mean(x * x, axis=-1, keepdims=True)
    xn = x * lax.rsqrt(ms + RMS_EPS) * g_ref[...]
    xh = xn.astype(BF16)
    xl = (xn - xh.astype(F32)).astype(BF16)
    w = wr_ref[...]
    wh = w.astype(BF16)
    wl = (w - wh.astype(F32)).astype(BF16)
    hi = jnp.dot(xh, jnp.concatenate([wh, wl], axis=1), preferred_element_type=F32)
    lo = jnp.dot(xl, wh, preferred_element_type=F32)
    logits = (hi[:, :LANES] + (hi[:, LANES:] + lo)) + br_ref[...]
    lane = lax.broadcasted_iota(jnp.int32, logits.shape, 1)
    ninf = jnp.float32(-jnp.inf)

    lg = jnp.where(lane < N_GROUPS, logits, ninf)
    mg = jnp.max(lg, axis=-1, keepdims=True)
    grp = jnp.min(jnp.where(lg == mg, lane, LANES), axis=-1, keepdims=True)
    p_grp = 1.0 / jnp.sum(jnp.exp(lg - mg), axis=-1, keepdims=True)

    lo = N_GROUPS + grp * EXPERTS_PER_GROUP
    le = jnp.where((lane >= lo) & (lane < lo + EXPERTS_PER_GROUP), logits, ninf)
    m1 = jnp.max(le, axis=-1, keepdims=True)
    i1 = jnp.min(jnp.where(le == m1, lane, LANES), axis=-1, keepdims=True)
    le2 = jnp.where(lane == i1, ninf, le)
    m2 = jnp.max(le2, axis=-1, keepdims=True)
    i2 = jnp.min(jnp.where(le2 == m2, lane, LANES), axis=-1, keepdims=True)
    z = jnp.sum(jnp.exp(le - m1), axis=-1, keepdims=True)
    p1 = 1.0 / z
    p2 = jnp.exp(m2 - m1) / z
    tot = p1 + p2
    g1 = p_grp * (p1 / tot)
    g2 = p_grp * (p2 / tot)
    gates_ref[...] = jnp.where(lane == 0, g1, jnp.where(lane == 1, g2, 0.0))

    e0 = i1 - N_GROUPS
    e1 = i2 - N_GROUPS
    tm = x.shape[0]
    onehot = jnp.where((lane == e0) | (lane == e1), 1.0, 0.0)
    tri = jnp.where(lax.broadcasted_iota(jnp.int32, (tm, tm), 0) > lax.broadcasted_iota(jnp.int32, (tm, tm), 1),
                    1.0, 0.0).astype(BF16)
    before = jnp.dot(tri, onehot.astype(BF16), preferred_element_type=F32) + carry[...]
    r0 = jnp.sum(jnp.where(lane == e0, before, 0.0), axis=-1, keepdims=True).astype(jnp.int32)
    r1 = jnp.sum(jnp.where(lane == e1, before, 0.0), axis=-1, keepdims=True).astype(jnp.int32)
    carry[...] = carry[...] + jnp.sum(onehot, axis=0, keepdims=True)
    counts_ref[...] = carry[...].astype(jnp.int32)
    ids_ref[...] = jnp.where(lane == 0, e0, jnp.where(lane == 1, e1, jnp.where(lane == 2, r0, jnp.where(lane == 3, r1, 0))))


def _router(h, gain, w_group, b_group, w_expert, b_expert):
    n, d = h.shape
    tm = min(256, n)
    wr = jnp.concatenate([w_group, w_expert.transpose(1, 0, 2).reshape(d, N_EXPERTS)], axis=1)
    wr = jnp.pad(wr.astype(F32), ((0, 0), (0, LANES - wr.shape[1])))
    br = jnp.concatenate([b_group, b_expert.reshape(N_EXPERTS)]).astype(F32)
    br = jnp.pad(br, (0, LANES - br.shape[0])).reshape(1, LANES)
    return pl.pallas_call(
        _router_body,
        grid=(n // tm,),
        in_specs=[pl.BlockSpec((tm, d), lambda i: (i, 0)),
                  pl.BlockSpec((1, d), lambda i: (0, 0)),
                  pl.BlockSpec((d, LANES), lambda i: (0, 0)),
                  pl.BlockSpec((1, LANES), lambda i: (0, 0))],
        out_specs=[pl.BlockSpec((tm, LANES), lambda i: (i, 0)),
                   pl.BlockSpec((tm, LANES), lambda i: (i, 0)),
                   pl.BlockSpec((1, LANES), lambda i: (0, 0))],
        out_shape=[jax.ShapeDtypeStruct((n, LANES), jnp.int32),
                   jax.ShapeDtypeStruct((n, LANES), F32),
                   jax.ShapeDtypeStruct((1, LANES), jnp.int32)],
        scratch_shapes=[pltpu.VMEM((1, LANES), F32)],
        compiler_params=_params("arbitrary"),
        name="norm_router",
    )(h, gain.reshape(1, d), wr, br)


def _dispatch_plan(ids, counts):
    n = ids.shape[0]
    nka = n * 2
    nb = (nka + MOE_ROWS - 1) // MOE_ROWS + N_EXPERTS
    e, rank = ids[:, :2], ids[:, 2:4]
    blocks = (counts + MOE_ROWS - 1) // MOE_ROWS
    blk_end = jnp.cumsum(blocks)
    blk_start = blk_end - blocks
    n_used = blk_end[-1]
    lanes = jnp.arange(N_EXPERTS, dtype=jnp.int32)
    row_start = jnp.sum(jnp.where(e[:, :, None] == lanes, blk_start * MOE_ROWS, 0), axis=-1)
    dest = (row_start + rank).astype(jnp.int32).reshape(nka)
    b = jnp.arange(nb, dtype=jnp.int32)
    block_e = jnp.minimum(jnp.sum(b[:, None] >= blk_end[None, :], axis=1), N_EXPERTS - 1).astype(jnp.int32)
    first = ((b == blk_start[block_e]) & (b < n_used)).astype(jnp.int32)
    nxt_blk = blk_end[block_e]
    nxt_e = jnp.where(nxt_blk < n_used, block_e[jnp.minimum(nxt_blk, nb - 1)], -1).astype(jnp.int32)
    return dest, block_e, first, nxt_e, n_used.astype(jnp.int32).reshape(1), nb


def _dispatch_body(dest_ref, h_ref, g_ref, xb_any, xb_hbm, buf, sem):
    del xb_any
    i = pl.program_id(0)
    n_steps = pl.num_programs(0)
    tm = h_ref.shape[0]
    slot = i % 2

    def row_copies(step, s, wait_only=False):
        for r in range(tm):
            for k in range(2):
                row = 0 if wait_only else dest_ref[(step * tm + r) * 2 + k]
                yield _row_copy(buf.at[s], r, xb_hbm, row, sem.at[s])

    @pl.when(i >= 2)
    def _():
        for c in row_copies(0, slot, wait_only=True):
            c.wait()

    x = h_ref[...]
    ms = jnp.mean(x * x, axis=-1, keepdims=True)
    xn = x * lax.rsqrt(ms + RMS_EPS) * g_ref[...]
    half_d = xn.shape[1] // 2
    buf[slot] = _pack_bf16_pair(xn[:, :half_d], xn[:, half_d:])
    for c in row_copies(i, slot):
        c.start()

    @pl.when(i == n_steps - 1)
    def _():
        for c in row_copies(0, slot, wait_only=True):
            c.wait()

        @pl.when(n_steps >= 2)
        def _():
            for c in row_copies(0, 1 - slot, wait_only=True):
                c.wait()


def _dispatch(h, gain, dest, xb_prev):
    n, d = h.shape
    tm = min(256, n)
    grid_spec = pltpu.PrefetchScalarGridSpec(
        num_scalar_prefetch=1,
        grid=(n // tm,),
        in_specs=[pl.BlockSpec((tm, d), lambda i, dest_: (i, 0)),
                  pl.BlockSpec((1, d), lambda i, dest_: (0, 0)),
                  pl.BlockSpec(memory_space=pl.ANY)],
        out_specs=pl.BlockSpec(memory_space=pl.ANY),
        scratch_shapes=[pltpu.VMEM((2, tm, d // 2), PACKED),
                        pltpu.SemaphoreType.DMA((2,))],
    )
    return pl.pallas_call(
        _dispatch_body,
        grid_spec=grid_spec,
        out_shape=jax.ShapeDtypeStruct(xb_prev.shape, PACKED),
        input_output_aliases={3: 0},
        compiler_params=_params("arbitrary"),
        name="moe_dispatch",
    )(dest, h, gain.reshape(1, d), xb_prev)


def _row_copy(src, src_row, dst, dst_row, sem):
    return pltpu.make_async_copy(src.at[pl.ds(src_row, 1), :], dst.at[pl.ds(dst_row, 1), :], sem)


def _experts_body(be_ref, first_ref, nxt_ref, nu_ref, x_ref, w1_hbm, w3_hbm, w2_hbm, o_ref,
                  st13, st2, wb13, wb2, wsem, *, layer):
    i = pl.program_id(0)
    n_used = nu_ref[0]
    rows = x_ref.shape[0]
    de = st2.shape[0]

    def weight_copies(e):
        return (pltpu.make_async_copy(w1_hbm.at[layer, e], st13.at[0], wsem.at[0]),
                pltpu.make_async_copy(w3_hbm.at[layer, e], st13.at[1], wsem.at[1]),
                pltpu.make_async_copy(w2_hbm.at[layer, e], st2, wsem.at[2]))

    @pl.when(i == 0)
    def _():
        for c in weight_copies(be_ref[0]):
            c.start()

    @pl.when(i < n_used)
    def _():
        @pl.when(first_ref[i] == 1)
        def _():
            for c in weight_copies(0):
                c.wait()
            wb13[:, :de] = st13[0].astype(BF16)
            wb13[:, de:] = st13[1].astype(BF16)
            wb2[...] = st2[...].astype(BF16)

            @pl.when(nxt_ref[i] >= 0)
            def _():
                for c in weight_copies(nxt_ref[i]):
                    c.start()

        x = jnp.concatenate(_unpack_bf16_pair(x_ref[...]), axis=1)
        half = rows // 2
        wide = (2 * de) // (2 * MXU_TILE) * (2 * MXU_TILE)
        h_a = jnp.dot(x, wb13[:, :wide], preferred_element_type=F32)
        h_b = jnp.concatenate(
            [jnp.dot(x[:half], wb13[:, wide:], preferred_element_type=F32),
             jnp.dot(x[half:], wb13[:, wide:], preferred_element_type=F32)], axis=0)
        h13 = jnp.concatenate([h_a, h_b], axis=1)
        h1, h3 = h13[:, :de], h13[:, de:]
        a = ((h1 / (1.0 + jnp.exp(-h1))) * h3).astype(BF16)
        half_d = wb2.shape[1] // 2
        cols = half_d // 2
        for c in range(2):
            lo = jnp.dot(a, wb2[:, c * cols:(c + 1) * cols], preferred_element_type=F32)
            hi = jnp.dot(a, wb2[:, half_d + c * cols:half_d + (c + 1) * cols], preferred_element_type=F32)
            o_ref[:, c * cols:(c + 1) * cols] = _pack_bf16_pair(lo, hi)

    @pl.when(i >= n_used)
    def _():
        o_ref[...] = jnp.zeros(o_ref.shape, o_ref.dtype)


def _experts(xb, block_e, first, nxt_e, n_used, nb, w1, w3, w2, *, layer):
    d, de = w1.shape[2:]
    assert xb.shape == (nb * MOE_ROWS, d // 2)
    grid_spec = pltpu.PrefetchScalarGridSpec(
        num_scalar_prefetch=4,
        grid=(nb,),
        in_specs=[pl.BlockSpec((MOE_ROWS, d // 2), lambda i, *_: (i, 0))] + [pl.BlockSpec(memory_space=pl.ANY)] * 3,
        out_specs=pl.BlockSpec((MOE_ROWS, d // 2), lambda i, *_: (i, 0)),
        scratch_shapes=[pltpu.VMEM((2, d, de), F32),
                        pltpu.VMEM((de, d), F32),
                        pltpu.VMEM((d, 2 * de), BF16),
                        pltpu.VMEM((de, d), BF16),
                        pltpu.SemaphoreType.DMA((3,))],
    )
    return pl.pallas_call(
        functools.partial(_experts_body, layer=layer),
        grid_spec=grid_spec,
        out_shape=jax.ShapeDtypeStruct((nb * MOE_ROWS, d // 2), PACKED),
        compiler_params=pltpu.CompilerParams(dimension_semantics=("arbitrary",),
                                             vmem_limit_bytes=EXPERTS_VMEM_LIMIT),
        name="experts",
    )(block_e, first, nxt_e, n_used, xb, w1, w3, w2)


def _combine_body(pos_ref, h_ref, g_ref, gain_ref, y_hbm, *rest, last):
    if last:
        n_ref, ybuf, sem = rest
    else:
        o_ref, n_ref, ybuf, sem = rest
    i = pl.program_id(0)
    n_steps = pl.num_programs(0)
    tm = h_ref.shape[0]

    def start_gather(step, slot):
        for r in range(tm):
            base = (step * tm + r) * 2
            _row_copy(y_hbm, pos_ref[base], ybuf.at[slot, 0], r, sem.at[slot]).start()
            _row_copy(y_hbm, pos_ref[base + 1], ybuf.at[slot, 1], r, sem.at[slot]).start()

    def wait_gather(slot):
        for r in range(tm):
            _row_copy(y_hbm, 0, ybuf.at[slot, 0], r, sem.at[slot]).wait()
            _row_copy(y_hbm, 0, ybuf.at[slot, 1], r, sem.at[slot]).wait()

    slot = i % 2

    @pl.when(i == 0)
    def _():
        start_gather(0, 0)

    @pl.when(i + 1 < n_steps)
    def _():
        start_gather(i + 1, 1 - slot)

    wait_gather(slot)
    g = g_ref[...]
    lo0, hi0 = _unpack_bf16_pair_f32(ybuf[slot, 0])
    lo1, hi1 = _unpack_bf16_pair_f32(ybuf[slot, 1])
    moe = jnp.concatenate([g[:, 0:1] * lo0 + g[:, 1:2] * lo1, g[:, 0:1] * hi0 + g[:, 1:2] * hi1], axis=1)
    out = h_ref[...] + moe
    if not last:
        o_ref[...] = out
    ms = jnp.mean(out * out, axis=-1, keepdims=True)
    n_ref[...] = (out * lax.rsqrt(ms + RMS_EPS) * gain_ref[...]).astype(n_ref.dtype)


def _combine(h, gates, pos, yb, next_gain, *, last):
    n, d = h.shape
    tm = min(256, n)
    row_spec = pl.BlockSpec((tm, d), lambda i, pos_: (i, 0))
    grid_spec = pltpu.PrefetchScalarGridSpec(
        num_scalar_prefetch=1,
        grid=(n // tm,),
        in_specs=[row_spec,
                  pl.BlockSpec((tm, LANES), lambda i, pos_: (i, 0)),
                  pl.BlockSpec((1, d), lambda i, pos_: (0, 0)),
                  pl.BlockSpec(memory_space=pl.ANY)],
        out_specs=[row_spec] if last else [row_spec, row_spec],
        scratch_shapes=[pltpu.VMEM((2, 2, tm, d // 2), PACKED),
                        pltpu.SemaphoreType.DMA((2,))],
    )
    normed = jax.ShapeDtypeStruct((n, d), F32 if last else BF16)
    return pl.pallas_call(
        functools.partial(_combine_body, last=last),
        grid_spec=grid_spec,
        out_shape=[normed] if last else [jax.ShapeDtypeStruct((n, d), F32), normed],
        compiler_params=_params("arbitrary"),
        name="moe_combine",
    )(pos, h, gates, next_gain.reshape(1, d), yb)


def _hierarchical_moe(h, gain, w_group, b_group, w_expert, b_expert, w1, w3, w2, next_gain, xb_prev, *, layer, last):
    ids, gates, counts = _router(h, gain, w_group, b_group, w_expert, b_expert)
    dest, block_e, first, nxt_e, n_used, nb = _dispatch_plan(ids[:, :4], counts[0, :N_EXPERTS])
    if xb_prev is None:
        xb_prev = jnp.zeros((nb * MOE_ROWS, h.shape[1] // 2), PACKED)
    xb = _dispatch(h, gain, dest, xb_prev)
    yb = _experts(xb, block_e, first, nxt_e, n_used, nb, w1, w3, w2, layer=layer)
    return _combine(h, gates, dest, yb, next_gain, last=last), xb


def kernel(x, norm_mix, norm_ffn, norm_final, a_w_in, a_q_gain, a_k_gain, a_w_o, b_w_in, b_v_gain, b_w_s, b_bias, b_w_o, c_w_in, c_sink, c_w_o, d_w_in, d_rpb, d_w_o, moe_w_group, moe_b_group, moe_w_expert, moe_b_expert, moe_w1, moe_w3, moe_w2):
    batch, seq, d = x.shape
    h = x.reshape(batch * seq, d)
    ones = jnp.ones((HEAD_DIM,), F32)
    depth = norm_mix.shape[0]
    hn = _rmsnorm(h, norm_mix[0], BF16)
    xb = None
    for i in range(depth):
        m, j = i % N_MIXERS, i // N_MIXERS
        if m == 0:
            cos, sin = _axial_tables(seq)
            qkv = _matmul_qkv(hn, a_w_in[j], cos, sin, a_q_gain[j], a_k_gain[j],
                              mode="axial", seq=seq)
            mixed, w_o = _attn_dense(qkv, seq), a_w_o[j]
        elif m == 1:
            z = _matmul_gelu(hn, b_w_in[j], F32)
            mixed, w_o = _sgu(z, b_v_gain[j], b_w_s[j], b_bias[j]), b_w_o[j]
        elif m == 2:
            cos, sin = _rope_tables(seq)
            qkv = _matmul_qkv(hn, c_w_in[j], cos, sin, ones, ones, mode="rope", seq=seq)
            mixed, w_o = _attn_window(qkv, c_sink[j].astype(F32), seq), c_w_o[j]
        else:
            cos, sin = _rope_tables(seq)
            qkv = _matmul_qkv(hn, d_w_in[j], cos, sin, ones, ones, mode="none", seq=seq)
            mixed, w_o = _attn_nbr(qkv, d_rpb[j], seq), d_w_o[j]
        h = _matmul_residual(mixed, w_o, h)
        last = i == depth - 1
        outs, xb = _hierarchical_moe(h, norm_ffn[i], moe_w_group[i], moe_b_group[i], moe_w_expert[i],
                                     moe_b_expert[i], moe_w1, moe_w3, moe_w2,
                                     norm_final if last else norm_mix[i + 1], xb, layer=i, last=last)
        if last:
            return outs[0].reshape(batch, seq, d)
        h, hn = outs
```
